```python
import jax, jax.numpy as jnp
from jax import lax
import numpy as np

D_MODEL = 2048
BATCH = 4
SEQ = 2048
DEPTH = 2
DEC_BATCH = 128
DEC_SEQ = 8
PAST_LEN = 16384
PAGE_SIZE = 128

D_POOL = D_MODEL // 4
D_SCONV = 3 * D_MODEL // 8
D_CCONV = D_MODEL - D_POOL - D_SCONV
POOL_WINDOWS = (2, 4, 8, 16)
N_POOL_GROUPS = len(POOL_WINDOWS)
POOL_GROUP = D_POOL // N_POOL_GROUPS
POOL_BUF = max(POOL_WINDOWS) - 1
SCONV_K = 3
CCONV_K = 31
FFN_CONV_K = 3
D_FF = 5632
D_IN = D_POOL + 3 * D_SCONV + 2 * D_CCONV
EPS = 1e-6

kernel_name = "hybrid_pool_shortconv_conformer_decoder_step"


def rmsnorm(x, g):
    xf = x.astype(jnp.float32)
    r = lax.rsqrt(jnp.mean(xf * xf, axis=-1, keepdims=True) + EPS)
    return (xf * r * g.astype(jnp.float32)).astype(x.dtype)


def causal_dwconv(buf, u, w):
    k = w.shape[0]
    c = u.shape[-1]
    ext = jnp.concatenate([buf.astype(u.dtype), u], axis=1)
    y = lax.conv_general_dilated(
        ext, w[:, None, :].astype(u.dtype), window_strides=(1,), padding='VALID',
        dimension_numbers=('NWC', 'WIO', 'NWC'), feature_group_count=c)
    return y, ext[:, ext.shape[1] - (k - 1):]


def multiscale_pool(buf, u, pos0, w_grp, scale):
    b, t, _ = u.shape
    ext = jnp.concatenate([buf.astype(u.dtype), u], axis=1)
    cs = jnp.cumsum(ext.astype(jnp.float32), axis=1)
    cs = jnp.pad(cs, ((0, 0), (1, 0), (0, 0)))
    pos = pos0 + jnp.arange(t, dtype=jnp.int32)
    outs = []
    for g, k in enumerate(POOL_WINDOWS):
        lo, hi = g * POOL_GROUP, (g + 1) * POOL_GROUP
        s = cs[:, POOL_BUF + 1:POOL_BUF + 1 + t, lo:hi] - cs[:, POOL_BUF + 1 - k:POOL_BUF + 1 - k + t, lo:hi]
        cnt = jnp.minimum(pos + 1, k).astype(jnp.float32)[None, :, None]
        outs.append(s / cnt)
    pooled = jnp.stack(outs, axis=2)
    d = pooled - u.astype(jnp.float32).reshape(b, t, N_POOL_GROUPS, POOL_GROUP)
    y = jnp.einsum('btgc,gcd->btgd', d, w_grp.astype(jnp.float32)).reshape(b, t, D_POOL)
    y = y * scale.astype(jnp.float32)
    return y.astype(u.dtype), ext[:, ext.shape[1] - POOL_BUF:]


def hybrid_layer(x, st_pool, st_sconv, st_cconv, st_ffn, pos0,
                 norm1_g, w_in, pool_w, pool_scale, sconv_w, cconv_w, cconv_b, cconv_norm_g,
                 w_out, norm2_g, w_up, ffn_conv_w, w_down):
    h = rmsnorm(x, norm1_g)
    proj = jnp.einsum('btd,de->bte', h, w_in.astype(h.dtype))
    o1 = D_POOL
    o2 = o1 + D_SCONV
    o3 = o2 + D_SCONV
    o4 = o3 + D_SCONV
    o5 = o4 + D_CCONV
    u_pool = proj[..., :o1]
    gate_b, gate_c, h_b = proj[..., o1:o2], proj[..., o2:o3], proj[..., o3:o4]
    glu_a, glu_g = proj[..., o4:o5], proj[..., o5:]

    y_pool, new_pool = multiscale_pool(st_pool, u_pool, pos0, pool_w, pool_scale)
    c_b, new_sconv = causal_dwconv(st_sconv, gate_c * h_b, sconv_w)
    y_sconv = gate_b * c_b
    v = glu_a * jax.nn.sigmoid(glu_g)
    c_c, new_cconv = causal_dwconv(st_cconv, v, cconv_w)
    y_cconv = jax.nn.silu(rmsnorm(c_c + cconv_b.astype(c_c.dtype), cconv_norm_g))

    mix = jnp.concatenate([y_pool, y_sconv, y_cconv], axis=-1)
    x = x + jnp.einsum('btd,de->bte', mix, w_out.astype(mix.dtype))

    h2 = rmsnorm(x, norm2_g)
    up = jnp.einsum('btd,df->btf', h2, w_up.astype(h2.dtype))
    up_c, new_ffn = causal_dwconv(st_ffn, up, ffn_conv_w)
    act = jax.nn.silu(up_c[..., :D_FF]) * up_c[..., D_FF:]
    x = x + jnp.einsum('btf,fd->btd', act, w_down.astype(act.dtype))
    return x, new_pool, new_sconv, new_cconv, new_ffn


def setup_inputs(seed: int = 0) -> dict:
    key = jax.random.key(seed)
    ks = jax.random.split(key, 24)
    f32 = jnp.float32
    nrm = lambda k, shape, s: jax.random.normal(k, shape, f32) * s
    return {
        "x_prompt": nrm(ks[0], (BATCH, SEQ, D_MODEL), 1.0),
        "x_sample": nrm(ks[1], (DEC_BATCH, DEC_SEQ, D_MODEL), 1.0),
        "state_pool": nrm(ks[2], (DEPTH, DEC_BATCH, POOL_BUF, D_POOL), 1.0),
        "state_sconv": nrm(ks[3], (DEPTH, DEC_BATCH, SCONV_K - 1, D_SCONV), 1.0),
        "state_cconv": nrm(ks[4], (DEPTH, DEC_BATCH, CCONV_K - 1, D_CCONV), 0.5),
        "state_ffn": nrm(ks[5], (DEPTH, DEC_BATCH, FFN_CONV_K - 1, 2 * D_FF), 1.0),
        "norm1_g": 1.0 + nrm(ks[6], (DEPTH, D_MODEL), 0.02),
        "w_in": nrm(ks[7], (DEPTH, D_MODEL, D_IN), D_MODEL ** -0.5),
        "pool_w": nrm(ks[8], (DEPTH, N_POOL_GROUPS, POOL_GROUP, POOL_GROUP), POOL_GROUP ** -0.5),
        "pool_scale": 1.0 + nrm(ks[9], (DEPTH, D_POOL), 0.02),
        "sconv_w": nrm(ks[10], (DEPTH, SCONV_K, D_SCONV), SCONV_K ** -0.5),
        "cconv_w": nrm(ks[11], (DEPTH, CCONV_K, D_CCONV), CCONV_K ** -0.5),
        "cconv_b": nrm(ks[12], (DEPTH, D_CCONV), 0.02),
        "cconv_norm_g": 1.0 + nrm(ks[13], (DEPTH, D_CCONV), 0.02),
        "w_out": nrm(ks[14], (DEPTH, D_MODEL, D_MODEL), D_MODEL ** -0.5),
        "norm2_g": 1.0 + nrm(ks[15], (DEPTH, D_MODEL), 0.02),
        "w_up": nrm(ks[16], (DEPTH, D_MODEL, 2 * D_FF), D_MODEL ** -0.5),
        "ffn_conv_w": nrm(ks[17], (DEPTH, FFN_CONV_K, 2 * D_FF), FFN_CONV_K ** -0.5),
        "w_down": nrm(ks[18], (DEPTH, D_FF, D_MODEL), D_FF ** -0.5),
        "final_norm_g": 1.0 + nrm(ks[19], (D_MODEL,), 0.02),
    }


def reference(x_prompt, x_sample, state_pool, state_sconv, state_cconv, state_ffn,
              norm1_g, w_in, pool_w, pool_scale, sconv_w, cconv_w, cconv_b, cconv_norm_g,
              w_out, norm2_g, w_up, ffn_conv_w, w_down, final_norm_g):
    dt_p = x_prompt.dtype
    zp_pool = jnp.zeros((BATCH, POOL_BUF, D_POOL), dt_p)
    zp_sconv = jnp.zeros((BATCH, SCONV_K - 1, D_SCONV), dt_p)
    zp_cconv = jnp.zeros((BATCH, CCONV_K - 1, D_CCONV), dt_p)
    zp_ffn = jnp.zeros((BATCH, FFN_CONV_K - 1, 2 * D_FF), dt_p)

    xp, xs = x_prompt, x_sample
    pp, sp, cp, fp = [], [], [], []
    ps, ss, cs_, fs = [], [], [], []
    for i in range(DEPTH):
        layer_w = (norm1_g[i], w_in[i], pool_w[i], pool_scale[i], sconv_w[i], cconv_w[i], cconv_b[i],
                   cconv_norm_g[i], w_out[i], norm2_g[i], w_up[i], ffn_conv_w[i], w_down[i])
        xp, a, b, c, d = hybrid_layer(xp, zp_pool, zp_sconv, zp_cconv, zp_ffn, 0, *layer_w)
        pp.append(a); sp.append(b); cp.append(c); fp.append(d)
        xs, a, b, c, d = hybrid_layer(xs, state_pool[i], state_sconv[i], state_cconv[i], state_ffn[i],
                                      PAST_LEN, *layer_w)
        ps.append(a); ss.append(b); cs_.append(c); fs.append(d)

    y_prompt = rmsnorm(xp, final_norm_g)
    y_sample = rmsnorm(xs, final_norm_g)
    new_pool_p = jnp.stack(pp, axis=0)
    new_sconv_p = jnp.stack(sp, axis=0)
    new_cconv_p = jnp.stack(cp, axis=0)
    new_ffn_p = jnp.stack(fp, axis=0)
    new_pool_s = jnp.stack(ps, axis=0)
    new_sconv_s = jnp.stack(ss, axis=0)
    new_cconv_s = jnp.stack(cs_, axis=0)
    new_ffn_s = jnp.stack(fs, axis=0)
    return (y_prompt, y_sample, new_pool_p, new_sconv_p, new_cconv_p, new_ffn_p,
            new_pool_s, new_sconv_s, new_cconv_s, new_ffn_s)
```

```python
import functools

import jax
import jax.numpy as jnp
from jax import lax
from jax.experimental import pallas as pl
from jax.experimental.pallas import tpu as pltpu

EPS = 1e-6
D_MODEL = 2048
D_POOL = 512
D_SCONV = 768
D_CCONV = 768
POOL_WINDOWS = (2, 4, 8, 16)
POOL_GROUP = 128
POOL_BUF = 15
SCONV_K = 3
CCONV_K = 31
FFN_CONV_K = 3
D_FF = 5632
D_IN = D_POOL + 3 * D_SCONV + 2 * D_CCONV
O1 = D_POOL
O2 = O1 + D_SCONV
O3 = O2 + D_SCONV
O4 = O3 + D_SCONV
O5 = O4 + D_CCONV

SUBLANES = 8
LANES = 128
MXU_N = 256
VMEM_LIMIT_BYTES = 56 * 1024 * 1024

MIX_TM_PROMPT = 256
SAMPLE_BLOCK_SEQS = 32
FFN_SAMPLE_BLOCK_SEQS = 64
FFN_TM_PROMPT = 512
FFN_CK = 512
CONV_ROW_BLOCK = 32

BF16 = jnp.bfloat16
F32 = jnp.float32


def _round_up(n, m):
    return (n + m - 1) // m * m


def _dot(a, b):
    return jnp.dot(a, b, preferred_element_type=F32)


def _rms_scale(x):
    return lax.rsqrt(jnp.mean(x * x, axis=-1, keepdims=True) + EPS)


def _mixer_body(*refs, tm, tstride, tiles_per_seq, has_state):
    it = iter(refs)
    x_ref = next(it)
    if has_state:
        st_pool_ref, st_sconv_ref, st_cconv_ref = next(it), next(it), next(it)
    g1_ref, w_in_ref, pool_w_ref, pool_scale_ref = next(it), next(it), next(it), next(it)
    sconv_w_ref, cconv_w_ref, cconv_b_ref, cconv_g_ref, w_out_ref = (
        next(it), next(it), next(it), next(it), next(it))
    o_ref, pool_o_ref, sconv_o_ref, cconv_o_ref = next(it), next(it), next(it), next(it)
    h_ref, mix_ref, ext_p_ref, ext_s_ref, ext_c_ref, cc_ref = (
        next(it), next(it), next(it), next(it), next(it), next(it))

    s = tstride
    hp = ext_p_ref.shape[0] - tm
    hs = ext_s_ref.shape[0] - tm
    hc = ext_c_ref.shape[0] - tm
    i = pl.program_id(0)

    if has_state:
        ext_p_ref[0:hp, :] = st_pool_ref[...]
        ext_s_ref[0:hs, :] = st_sconv_ref[...]
        ext_c_ref[0:hc, :] = st_cconv_ref[...]
    else:
        @pl.when(i % tiles_per_seq == 0)
        def _():
            ext_p_ref[0:hp, :] = jnp.zeros((hp, D_POOL), F32)
            ext_s_ref[0:hs, :] = jnp.zeros((hs, D_SCONV), F32)
            ext_c_ref[0:hc, :] = jnp.zeros((hc, D_CCONV), F32)

    x = x_ref[...]
    h_ref[...] = (x * _rms_scale(x) * g1_ref[...]).astype(BF16)

    if not has_state:
        pos = (i % tiles_per_seq) * tm + lax.broadcasted_iota(jnp.int32, (tm, 1), 0)
    for g, k in enumerate(POOL_WINDOWS):
        ln = slice(g * POOL_GROUP, (g + 1) * POOL_GROUP)
        u = _dot(h_ref[...], w_in_ref[:, ln])
        ext_p_ref[hp:hp + tm, ln] = u
        win = u
        for m in range(1, k):
            win = win + ext_p_ref[hp - m * s:hp - m * s + tm, ln]
        if has_state:
            pooled = win / float(k)
        else:
            pooled = win / jnp.minimum(pos + 1, k).astype(F32)
        d = (pooled - u).astype(BF16)
        y = _dot(d, pool_w_ref[g]) * pool_scale_ref[:, ln]
        mix_ref[:, ln] = y.astype(BF16)

    for c in range(D_SCONV // MXU_N):
        ln = slice(c * MXU_N, (c + 1) * MXU_N)
        gate_c = _dot(h_ref[...], w_in_ref[:, O2 + c * MXU_N:O2 + (c + 1) * MXU_N])
        h_b = _dot(h_ref[...], w_in_ref[:, O3 + c * MXU_N:O3 + (c + 1) * MXU_N])
        q = gate_c * h_b
        ext_s_ref[hs:hs + tm, ln] = q
        conv = sconv_w_ref[SCONV_K - 1:SCONV_K, ln] * q
        for t in range(SCONV_K - 1):
            back = (SCONV_K - 1 - t) * s
            conv = conv + sconv_w_ref[t:t + 1, ln] * ext_s_ref[hs - back:hs - back + tm, ln]
        gate_b = _dot(h_ref[...], w_in_ref[:, O1 + c * MXU_N:O1 + (c + 1) * MXU_N])
        mix_ref[:, O1 + c * MXU_N:O1 + (c + 1) * MXU_N] = (gate_b * conv).astype(BF16)

    for c in range(D_CCONV // MXU_N):
        ln = slice(c * MXU_N, (c + 1) * MXU_N)
        glu_a = _dot(h_ref[...], w_in_ref[:, O4 + c * MXU_N:O4 + (c + 1) * MXU_N])
        glu_g = _dot(h_ref[...], w_in_ref[:, O5 + c * MXU_N:O5 + (c + 1) * MXU_N])
        ext_c_ref[hc:hc + tm, ln] = glu_a * jax.nn.sigmoid(glu_g)
    for lt in range(D_CCONV // LANES):
        ln = slice(lt * LANES, (lt + 1) * LANES)
        for rb in range(tm // CONV_ROW_BLOCK):
            r0 = rb * CONV_ROW_BLOCK
            acc = jnp.broadcast_to(cconv_b_ref[:, ln], (CONV_ROW_BLOCK, LANES))
            for t in range(CCONV_K):
                start = hc - (CCONV_K - 1 - t) * s + r0
                acc = acc + cconv_w_ref[t:t + 1, ln] * ext_c_ref[start:start + CONV_ROW_BLOCK, ln]
            cc_ref[r0:r0 + CONV_ROW_BLOCK, ln] = acc
    cc = cc_ref[...]
    z = cc * _rms_scale(cc) * cconv_g_ref[...]
    mix_ref[:, O2:D_MODEL] = (z * jax.nn.sigmoid(z)).astype(BF16)

    o_ref[...] = x_ref[...] + _dot(mix_ref[...], w_out_ref[...])

    nb_p, nb_s, nb_c = POOL_BUF * s, (SCONV_K - 1) * s, (CCONV_K - 1) * s
    new_p = ext_p_ref[hp + tm - nb_p:hp + tm, :]
    new_s = ext_s_ref[hs + tm - nb_s:hs + tm, :]
    new_c = ext_c_ref[hc + tm - nb_c:hc + tm, :]
    if has_state:
        pool_o_ref[...] = new_p
        sconv_o_ref[...] = new_s
        cconv_o_ref[...] = new_c
    else:
        pool_o_ref[0] = new_p
        sconv_o_ref[0] = new_s
        cconv_o_ref[0] = new_c
        ext_p_ref[0:hp, :] = ext_p_ref[tm:tm + hp, :]
        ext_s_ref[0:hs, :] = ext_s_ref[tm:tm + hs, :]
        ext_c_ref[0:hc, :] = ext_c_ref[tm:tm + hc, :]


def _const_spec(shape, single_buffer=False):
    idx = lambda i: (0,) * len(shape)
    if single_buffer:
        return pl.BlockSpec(shape, idx, pipeline_mode=pl.Buffered(1))
    return pl.BlockSpec(shape, idx)


def _mixer_call(x, states, w, *, tm, tstride, tiles_per_seq, n_seq):
    m = x.shape[0]
    has_state = states is not None
    n_tiles = m // tm
    if has_state:
        halo = [POOL_BUF * tstride, (SCONV_K - 1) * tstride, (CCONV_K - 1) * tstride]
    else:
        halo = [_round_up(POOL_BUF, SUBLANES), _round_up(SCONV_K - 1, SUBLANES),
                _round_up(CCONV_K - 1, SUBLANES)]
    widths = [D_POOL, D_SCONV, D_CCONV]
    bufs = [POOL_BUF, SCONV_K - 1, CCONV_K - 1]

    in_specs = [pl.BlockSpec((tm, D_MODEL), lambda i: (i, 0))]
    args = [x]
    if has_state:
        for st, hl, wd in zip(states, halo, widths):
            in_specs.append(pl.BlockSpec((hl, wd), lambda i: (i, 0),
                                         pipeline_mode=pl.Buffered(1)))
            args.append(st)
    in_specs += [
        _const_spec((1, D_MODEL)),
        _const_spec((D_MODEL, D_IN), single_buffer=True),
        _const_spec((len(POOL_WINDOWS), POOL_GROUP, POOL_GROUP)),
        _const_spec((1, D_POOL)),
        _const_spec((SCONV_K, D_SCONV)),
        _const_spec((CCONV_K, D_CCONV)),
        _const_spec((1, D_CCONV)),
        _const_spec((1, D_CCONV)),
        _const_spec((D_MODEL, D_MODEL), single_buffer=True),
    ]
    args += [w["norm1_g"], w["w_in"], w["pool_w"], w["pool_scale"], w["sconv_w"],
             w["cconv_w"], w["cconv_b"], w["cconv_norm_g"], w["w_out"]]

    out_shape = [jax.ShapeDtypeStruct((m, D_MODEL), F32)]
    out_specs = [pl.BlockSpec((tm, D_MODEL), lambda i: (i, 0))]
    for hl, wd, nb in zip(halo, widths, bufs):
        if has_state:
            out_shape.append(jax.ShapeDtypeStruct((n_tiles * hl, wd), F32))
            out_specs.append(pl.BlockSpec((hl, wd), lambda i: (i, 0)))
        else:
            out_shape.append(jax.ShapeDtypeStruct((n_seq, nb, wd), F32))
            out_specs.append(pl.BlockSpec((1, nb, wd), lambda i: (i // tiles_per_seq, 0, 0)))

    scratch = [
        pltpu.VMEM((tm, D_MODEL), BF16),
        pltpu.VMEM((tm, D_MODEL), BF16),
        pltpu.VMEM((halo[0] + tm, D_POOL), F32),
        pltpu.VMEM((halo[1] + tm, D_SCONV), F32),
        pltpu.VMEM((halo[2] + tm, D_CCONV), F32),
        pltpu.VMEM((tm, D_CCONV), F32),
    ]
    body = functools.partial(_mixer_body, tm=tm, tstride=tstride,
                             tiles_per_seq=tiles_per_seq, has_state=has_state)
    return pl.pallas_call(
        body,
        grid=(n_tiles,),
        in_specs=in_specs,
        out_specs=out_specs,
        out_shape=out_shape,
        scratch_shapes=scratch,
        compiler_params=pltpu.CompilerParams(
            dimension_semantics=("arbitrary",), vmem_limit_bytes=VMEM_LIMIT_BYTES),
        name="mixer_state" if has_state else "mixer_prompt",
    )(*args)


def _ffn_body(*refs, tm, tstride, tiles_per_seq, has_state, final_norm):
    it = iter(refs)
    x_ref = next(it)
    if has_state:
        st_a_ref, st_g_ref = next(it), next(it)
    g2_ref, wa_ref, wg_ref, cwa_ref, cwg_ref, wd_ref = (
        next(it), next(it), next(it), next(it), next(it), next(it))
    if final_norm:
        gf_ref = next(it)
    o_ref, new_a_ref, new_g_ref = next(it), next(it), next(it)
    h_ref, ext_ref = next(it), next(it)
    if not has_state:
        halo_ref = next(it)

    s = tstride
    ck = wa_ref.shape[1]
    hl = ext_ref.shape[0] - tm
    i = pl.program_id(0)
    j = pl.program_id(1)
    nj = pl.num_programs(1)

    @pl.when(j == 0)
    def _():
        x = x_ref[...]
        h_ref[...] = (x * _rms_scale(x) * g2_ref[...]).astype(BF16)
        o_ref[...] = x
        if not has_state:
            @pl.when(i % tiles_per_seq == 0)
            def _():
                halo_ref[...] = jnp.zeros(halo_ref.shape, F32)

    if has_state:
        ext_ref[0:hl, 0:ck] = st_a_ref[...]
        ext_ref[0:hl, ck:2 * ck] = st_g_ref[...]
    else:
        ext_ref[0:hl, :] = halo_ref[j]

    up_a = _dot(h_ref[...], wa_ref[...])
    up_g = _dot(h_ref[...], wg_ref[...])
    ext_ref[hl:hl + tm, 0:ck] = up_a
    ext_ref[hl:hl + tm, ck:2 * ck] = up_g

    def conv(up, cw_ref, lo):
        out = cw_ref[FFN_CONV_K - 1:FFN_CONV_K, :] * up
        for t in range(FFN_CONV_K - 1):
            back = (FFN_CONV_K - 1 - t) * s
            out = out + cw_ref[t:t + 1, :] * ext_ref[hl - back:hl - back + tm, lo:lo + ck]
        return out

    c_a = conv(up_a, cwa_ref, 0)
    c_g = conv(up_g, cwg_ref, ck)
    act = (c_a * jax.nn.sigmoid(c_a) * c_g).astype(BF16)
    o_ref[...] += _dot(act, wd_ref[...])

    nb = (FFN_CONV_K - 1) * s
    new_a = ext_ref[hl + tm - nb:hl + tm, 0:ck]
    new_g = ext_ref[hl + tm - nb:hl + tm, ck:2 * ck]
    if has_state:
        new_a_ref[...] = new_a
        new_g_ref[...] = new_g
    else:
        seq = i // tiles_per_seq
        new_a_ref[seq, j] = new_a
        new_g_ref[seq, j] = new_g
        halo_ref[j] = ext_ref[tm:tm + hl, :]

    if final_norm:
        @pl.when(j == nj - 1)
        def _():
            y = o_ref[...]
            o_ref[...] = y * _rms_scale(y) * gf_ref[...]


def _ffn_call(x, states, w, final_g, *, tm, tstride, tiles_per_seq, n_seq):
    m = x.shape[0]
    has_state = states is not None
    final_norm = final_g is not None
    n_tiles = m // tm
    ck = FFN_CK
    nj = D_FF // ck
    nb = FFN_CONV_K - 1
    hl = nb * tstride if has_state else _round_up(nb, SUBLANES)

    in_specs = [pl.BlockSpec((tm, D_MODEL), lambda i, j: (i, 0))]
    args = [x]
    if has_state:
        in_specs += [pl.BlockSpec((hl, ck), lambda i, j: (i, j)),
                     pl.BlockSpec((hl, ck), lambda i, j: (i, nj + j))]
        args += [states, states]
    in_specs += [
        pl.BlockSpec((1, D_MODEL), lambda i, j: (0, 0)),
        pl.BlockSpec((D_MODEL, ck), lambda i, j: (0, j)),
        pl.BlockSpec((D_MODEL, ck), lambda i, j: (0, nj + j)),
        pl.BlockSpec((FFN_CONV_K, ck), lambda i, j: (0, j)),
        pl.BlockSpec((FFN_CONV_K, ck), lambda i, j: (0, nj + j)),
        pl.BlockSpec((ck, D_MODEL), lambda i, j: (j, 0)),
    ]
    args += [w["norm2_g"], w["w_up"], w["w_up"], w["ffn_conv_w"], w["ffn_conv_w"], w["w_down"]]
    if final_norm:
        in_specs.append(pl.BlockSpec((1, D_MODEL), lambda i, j: (0, 0)))
        args.append(final_g)

    out_shape = [jax.ShapeDtypeStruct((m, D_MODEL), F32)]
    out_specs = [pl.BlockSpec((tm, D_MODEL), lambda i, j: (i, 0))]
    for _ in range(2):
        if has_state:
            out_shape.append(jax.ShapeDtypeStruct((n_tiles * hl, D_FF), F32))
            out_specs.append(pl.BlockSpec((hl, ck), lambda i, j: (i, j)))
        else:
            out_shape.append(jax.ShapeDtypeStruct((n_seq, nj, nb, ck), F32))
            out_specs.append(pl.BlockSpec((n_seq, nj, nb, ck), lambda i, j: (0, 0, 0, 0)))

    scratch = [pltpu.VMEM((tm, D_MODEL), BF16), pltpu.VMEM((hl + tm, 2 * ck), F32)]
    if not has_state:
        scratch.append(pltpu.VMEM((nj, hl, 2 * ck), F32))
    body = functools.partial(_ffn_body, tm=tm, tstride=tstride, tiles_per_seq=tiles_per_seq,
                             has_state=has_state, final_norm=final_norm)
    return pl.pallas_call(
        body,
        grid=(n_tiles, nj),
        in_specs=in_specs,
        out_specs=out_specs,
        out_shape=out_shape,
        scratch_shapes=scratch,
        compiler_params=pltpu.CompilerParams(
            dimension_semantics=("arbitrary", "arbitrary"), vmem_limit_bytes=VMEM_LIMIT_BYTES),
        name="ffn_state" if has_state else "ffn_prompt",
    )(*args)


def _to_blocked(a, bs):
    b, t, c = a.shape
    return a.reshape(b // bs, bs, t, c).transpose(0, 2, 1, 3).reshape(b * t, c)


def _from_blocked(a, bs, t):
    n, c = a.shape
    b = n // t
    return a.reshape(b // bs, t, bs, c).transpose(0, 2, 1, 3).reshape(b, t, c)


def _reblock(a, t, bs_from, bs_to):
    return _to_blocked(_from_blocked(a, bs_from, t), bs_to)


def kernel(x_prompt, x_sample, state_pool, state_sconv, state_cconv, state_ffn, norm1_g, w_in,
           pool_w, pool_scale, sconv_w, cconv_w, cconv_b, cconv_norm_g, w_out, norm2_g, w_up,
           ffn_conv_w, w_down, final_norm_g):
    batch, seq, _ = x_prompt.shape
    dec_batch, dec_seq, _ = x_sample.shape
    depth = w_in.shape[0]
    bs = SAMPLE_BLOCK_SEQS
    fbs = FFN_SAMPLE_BLOCK_SEQS

    xp = x_prompt.reshape(batch * seq, D_MODEL)
    xs = _to_blocked(x_sample, bs)
    final_g = final_norm_g.reshape(1, D_MODEL)

    outs_p = [[], [], [], []]
    outs_s = [[], [], [], []]
    for l in range(depth):
        w = {
            "norm1_g": norm1_g[l].reshape(1, D_MODEL),
            "w_in": w_in[l].astype(BF16),
            "pool_w": pool_w[l].astype(BF16),
            "pool_scale": pool_scale[l].reshape(1, D_POOL),
            "sconv_w": sconv_w[l],
            "cconv_w": cconv_w[l],
            "cconv_b": cconv_b[l].reshape(1, D_CCONV),
            "cconv_norm_g": cconv_norm_g[l].reshape(1, D_CCONV),
            "w_out": w_out[l].astype(BF16),
            "norm2_g": norm2_g[l].reshape(1, D_MODEL),
            "w_up": w_up[l].astype(BF16),
            "ffn_conv_w": ffn_conv_w[l],
            "w_down": w_down[l].astype(BF16),
        }
        last = final_g if l == depth - 1 else None

        xp, pool_p, sconv_p, cconv_p = _mixer_call(
            xp, None, w, tm=MIX_TM_PROMPT, tstride=1,
            tiles_per_seq=seq // MIX_TM_PROMPT, n_seq=batch)
        xp, ffa_p, ffg_p = _ffn_call(
            xp, None, w, last, tm=FFN_TM_PROMPT, tstride=1,
            tiles_per_seq=seq // FFN_TM_PROMPT, n_seq=batch)
        outs_p[0].append(pool_p)
        outs_p[1].append(sconv_p)
        outs_p[2].append(cconv_p)
        unchunk = lambda a: a.transpose(0, 2, 1, 3).reshape(batch, FFN_CONV_K - 1, D_FF)
        outs_p[3].append(jnp.concatenate([unchunk(ffa_p), unchunk(ffg_p)], axis=-1))

        st = [_to_blocked(state_pool[l], bs), _to_blocked(state_sconv[l], bs),
              _to_blocked(state_cconv[l], bs)]
        xs, pool_s, sconv_s, cconv_s = _mixer_call(
            xs, st, w, tm=dec_seq * bs, tstride=bs, tiles_per_seq=1, n_seq=dec_batch)
        xs_f = _reblock(xs, dec_seq, bs, fbs)
        xs_f, ffa_s, ffg_s = _ffn_call(
            xs_f, _to_blocked(state_ffn[l], fbs), w, last,
            tm=dec_seq * fbs, tstride=fbs, tiles_per_seq=1, n_seq=dec_batch)
        xs = _reblock(xs_f, dec_seq, fbs, bs)
        outs_s[0].append(_from_blocked(pool_s, bs, POOL_BUF))
        outs_s[1].append(_from_blocked(sconv_s, bs, SCONV_K - 1))
        outs_s[2].append(_from_blocked(cconv_s, bs, CCONV_K - 1))
        outs_s[3].append(_from_blocked(jnp.concatenate([ffa_s, ffg_s], axis=-1),
                                       fbs, FFN_CONV_K - 1))

    y_prompt = xp.reshape(batch, seq, D_MODEL)
    y_sample = _from_blocked(xs, bs, dec_seq)
    return (y_prompt, y_sample,
            jnp.stack(outs_p[0]), jnp.stack(outs_p[1]), jnp.stack(outs_p[2]), jnp.stack(outs_p[3]),
            jnp.stack(outs_s[0]), jnp.stack(outs_s[1]), jnp.stack(outs_s[2]), jnp.stack(outs_s[3]))
```

```python
import functools

import jax
import jax.numpy as jnp
from jax import lax
from jax.experimental import pallas as pl
from jax.experimental.pallas import tpu as pltpu

EPS = 1e-6
D_MODEL = 2048
D_POOL = 512
D_SCONV = 768
D_CCONV = 768
POOL_WINDOWS = (2, 4, 8, 16)
POOL_GROUP = 128
POOL_BUF = 15
SCONV_K = 3
CCONV_K = 31
FFN_CONV_K = 3
D_FF = 5632
D_IN = D_POOL + 3 * D_SCONV + 2 * D_CCONV
O1 = D_POOL
O2 = O1 + D_SCONV
O3 = O2 + D_SCONV
O4 = O3 + D_SCONV
O5 = O4 + D_CCONV

SUBLANES = 8
LANES = 128
MXU_N = 256
VMEM_LIMIT_BYTES = 56 * 1024 * 1024

MIX_TM_PROMPT = 256
MIX_SAMPLE_SEQS = 32
FFN_TM_PROMPT = 1024
FFN_CK = 256
CONV_ROW_BLOCK = 32

BF16 = jnp.bfloat16
F32 = jnp.float32


def _round_up(n, m):
    return (n + m - 1) // m * m


def _dot(a, b):
    return jnp.dot(a, b, preferred_element_type=F32)


def _rms_scale(x):
    return lax.rsqrt(jnp.mean(x * x, axis=-1, keepdims=True) + EPS)


def _rows(ref):
    v = ref[...]
    return v.reshape(-1, v.shape[-1])


def _store_rows(ref, v):
    ref[...] = v.reshape(ref.shape)


def _conv_aligned_stride(ext_ref, w_ref, b_ref, out_ref, ln, halo, tm, tstride):
    k = w_ref.shape[0]
    for rb in range(tm // CONV_ROW_BLOCK):
        r0 = rb * CONV_ROW_BLOCK
        acc = jnp.broadcast_to(b_ref[:, ln], (CONV_ROW_BLOCK, LANES))
        for t in range(k):
            start = halo - (k - 1 - t) * tstride + r0
            acc = acc + w_ref[t:t + 1, ln] * ext_ref[start:start + CONV_ROW_BLOCK, ln]
        out_ref[r0:r0 + CONV_ROW_BLOCK, ln] = acc


def _conv_unit_stride(ext_ref, w_ref, b_ref, out_ref, ln, halo, tm):
    k = w_ref.shape[0]
    n_a = -(-k // SUBLANES)
    wb = [jnp.broadcast_to(w_ref[k - 1 - d:k - d, ln], (SUBLANES, LANES)) for d in range(k)]
    bias = jnp.broadcast_to(b_ref[:, ln], (SUBLANES, LANES))
    row = lax.broadcasted_iota(jnp.int32, (SUBLANES, LANES), 0)
    first = halo // SUBLANES
    tiles = {}

    def tile(q):
        if q not in tiles:
            tiles[q] = ext_ref[q * SUBLANES:(q + 1) * SUBLANES, ln]
        return tiles[q]

    prev = None
    for p in range(first - 1, first + tm // SUBLANES):
        rolled = []
        for r in range(SUBLANES):
            z = None
            for a in range(n_a):
                d = SUBLANES * a + r
                if d < k:
                    term = wb[d] * tile(p - a)
                    z = term if z is None else z + term
            rolled.append(z if r == 0 else pltpu.roll(z, r, axis=0))
        if p >= first:
            y = bias + rolled[0]
            for r in range(1, SUBLANES):
                y = y + jnp.where(row >= r, rolled[r], prev[r])
            o0 = (p - first) * SUBLANES
            out_ref[o0:o0 + SUBLANES, ln] = y
        prev = rolled
        tiles.pop(p - n_a + 1, None)


def _mixer_body(*refs, tm, tstride, tiles_per_seq, has_state):
    it = iter(refs)
    x_ref = next(it)
    if has_state:
        st_pool_ref, st_sconv_ref, st_cconv_ref = next(it), next(it), next(it)
    g1_ref, w_in_ref, pool_w_ref, pool_scale_ref = next(it), next(it), next(it), next(it)
    sconv_w_ref, cconv_w_ref, cconv_b_ref, cconv_g_ref, w_out_ref = (
        next(it), next(it), next(it), next(it), next(it))
    o_ref, pool_o_ref, sconv_o_ref, cconv_o_ref = next(it), next(it), next(it), next(it)
    h_ref, mix_ref, ext_p_ref, ext_s_ref, ext_c_ref, cc_ref = (
        next(it), next(it), next(it), next(it), next(it), next(it))

    s = tstride
    hp = ext_p_ref.shape[0] - tm
    hs = ext_s_ref.shape[0] - tm
    hc = ext_c_ref.shape[0] - tm
    i = pl.program_id(0)

    if has_state:
        ext_p_ref[0:hp, :] = _rows(st_pool_ref)
        ext_s_ref[0:hs, :] = _rows(st_sconv_ref)
        ext_c_ref[0:hc, :] = _rows(st_cconv_ref)
    else:
        @pl.when(i % tiles_per_seq == 0)
        def _():
            ext_p_ref[0:hp, :] = jnp.zeros((hp, D_POOL), F32)
            ext_s_ref[0:hs, :] = jnp.zeros((hs, D_SCONV), F32)
            ext_c_ref[0:hc, :] = jnp.zeros((hc, D_CCONV), F32)

    x = _rows(x_ref)
    h_ref[...] = (x * _rms_scale(x) * g1_ref[...]).astype(BF16)

    def glu(c):
        ln = slice(c * MXU_N, (c + 1) * MXU_N)
        glu_a = _dot(h_ref[...], w_in_ref[:, O4 + c * MXU_N:O4 + (c + 1) * MXU_N])
        glu_g = _dot(h_ref[...], w_in_ref[:, O5 + c * MXU_N:O5 + (c + 1) * MXU_N])
        ext_c_ref[hc:hc + tm, ln] = glu_a * jax.nn.sigmoid(glu_g)

    def conv31(lt):
        ln = slice(lt * LANES, (lt + 1) * LANES)
        if s == 1:
            _conv_unit_stride(ext_c_ref, cconv_w_ref, cconv_b_ref, cc_ref, ln, hc, tm)
        else:
            _conv_aligned_stride(ext_c_ref, cconv_w_ref, cconv_b_ref, cc_ref, ln, hc, tm, s)

    glu(0)
    glu(1)
    conv31(0)
    conv31(1)
    glu(2)
    conv31(2)
    conv31(3)

    if not has_state:
        pos = (i % tiles_per_seq) * tm + lax.broadcasted_iota(jnp.int32, (tm, 1), 0)
    for g, k in enumerate(POOL_WINDOWS):
        ln = slice(g * POOL_GROUP, (g + 1) * POOL_GROUP)
        u = _dot(h_ref[...], w_in_ref[:, ln])
        ext_p_ref[hp:hp + tm, ln] = u
        win = u
        for m in range(1, k):
            win = win + ext_p_ref[hp - m * s:hp - m * s + tm, ln]
        if has_state:
            pooled = win / float(k)
        else:
            pooled = win / jnp.minimum(pos + 1, k).astype(F32)
        d = (pooled - u).astype(BF16)
        y = _dot(d, pool_w_ref[g]) * pool_scale_ref[:, ln]
        mix_ref[:, ln] = y.astype(BF16)

    for c in range(D_SCONV // MXU_N):
        if 4 + c < D_CCONV // LANES:
            conv31(4 + c)
        ln = slice(c * MXU_N, (c + 1) * MXU_N)
        gate_c = _dot(h_ref[...], w_in_ref[:, O2 + c * MXU_N:O2 + (c + 1) * MXU_N])
        h_b = _dot(h_ref[...], w_in_ref[:, O3 + c * MXU_N:O3 + (c + 1) * MXU_N])
        q = gate_c * h_b
        ext_s_ref[hs:hs + tm, ln] = q
        conv = sconv_w_ref[SCONV_K - 1:SCONV_K, ln] * q
        for t in range(SCONV_K - 1):
            back = (SCONV_K - 1 - t) * s
            conv = conv + sconv_w_ref[t:t + 1, ln] * ext_s_ref[hs - back:hs - back + tm, ln]
        gate_b = _dot(h_ref[...], w_in_ref[:, O1 + c * MXU_N:O1 + (c + 1) * MXU_N])
        mix_ref[:, O1 + c * MXU_N:O1 + (c + 1) * MXU_N] = (gate_b * conv).astype(BF16)

    cc = cc_ref[...]
    z = cc * _rms_scale(cc) * cconv_g_ref[...]
    mix_ref[:, O2:D_MODEL] = (z * jax.nn.sigmoid(z)).astype(BF16)

    _store_rows(o_ref, _rows(x_ref) + _dot(mix_ref[...], w_out_ref[...]))

    nb_p, nb_s, nb_c = POOL_BUF * s, (SCONV_K - 1) * s, (CCONV_K - 1) * s
    new_p = ext_p_ref[hp + tm - nb_p:hp + tm, :]
    new_s = ext_s_ref[hs + tm - nb_s:hs + tm, :]
    new_c = ext_c_ref[hc + tm - nb_c:hc + tm, :]
    if has_state:
        _store_rows(pool_o_ref, new_p)
        _store_rows(sconv_o_ref, new_s)
        _store_rows(cconv_o_ref, new_c)
    else:
        pool_o_ref[0] = new_p
        sconv_o_ref[0] = new_s
        cconv_o_ref[0] = new_c
        ext_p_ref[0:hp, :] = ext_p_ref[tm:tm + hp, :]
        ext_s_ref[0:hs, :] = ext_s_ref[tm:tm + hs, :]
        ext_c_ref[0:hc, :] = ext_c_ref[tm:tm + hc, :]


def _layer_spec(shape, layer, single_buffer=False):
    idx = lambda *_: (layer,) + (0,) * len(shape)
    if single_buffer:
        return pl.BlockSpec((None,) + shape, idx, pipeline_mode=pl.Buffered(1))
    return pl.BlockSpec((None,) + shape, idx)


def _mixer_call(x, states, w, layer, *, tm, tstride, tiles_per_seq, n_seq):
    has_state = states is not None
    widths = [D_POOL, D_SCONV, D_CCONV]
    bufs = [POOL_BUF, SCONV_K - 1, CCONV_K - 1]
    if has_state:
        t_steps, n_b, _ = x.shape
        n_tiles = n_b // tstride
        halo = [nb * tstride for nb in bufs]
        x_spec = pl.BlockSpec((t_steps, tstride, D_MODEL), lambda i: (0, i, 0))
    else:
        n_tiles = x.shape[0] // tm
        halo = [_round_up(nb, SUBLANES) for nb in bufs]
        x_spec = pl.BlockSpec((tm, D_MODEL), lambda i: (i, 0))

    in_specs = [x_spec]
    args = [x]
    if has_state:
        for st, nb, wd in zip(states, bufs, widths):
            in_specs.append(pl.BlockSpec((None, nb, tstride, wd), lambda i: (layer, 0, i, 0),
                                         pipeline_mode=pl.Buffered(1)))
            args.append(st)
    in_specs += [
        _layer_spec((1, D_MODEL), layer),
        _layer_spec((D_MODEL, D_IN), layer, single_buffer=True),
        _layer_spec((len(POOL_WINDOWS), POOL_GROUP, POOL_GROUP), layer),
        _layer_spec((1, D_POOL), layer),
        _layer_spec((SCONV_K, D_SCONV), layer),
        _layer_spec((CCONV_K, D_CCONV), layer),
        _layer_spec((1, D_CCONV), layer),
        _layer_spec((1, D_CCONV), layer),
        _layer_spec((D_MODEL, D_MODEL), layer, single_buffer=True),
    ]
    args += [w["norm1_g"], w["w_in"], w["pool_w"], w["pool_scale"], w["sconv_w"],
             w["cconv_w"], w["cconv_b"], w["cconv_norm_g"], w["w_out"]]

    out_shape = [jax.ShapeDtypeStruct(x.shape, F32)]
    out_specs = [x_spec]
    for nb, wd in zip(bufs, widths):
        if has_state:
            out_shape.append(jax.ShapeDtypeStruct((nb, n_b, wd), F32))
            out_specs.append(pl.BlockSpec((nb, tstride, wd), lambda i: (0, i, 0)))
        else:
            out_shape.append(jax.ShapeDtypeStruct((n_seq, nb, wd), F32))
            out_specs.append(pl.BlockSpec((1, nb, wd), lambda i: (i // tiles_per_seq, 0, 0)))

    scratch = [
        pltpu.VMEM((tm, D_MODEL), BF16),
        pltpu.VMEM((tm, D_MODEL), BF16),
        pltpu.VMEM((halo[0] + tm, D_POOL), F32),
        pltpu.VMEM((halo[1] + tm, D_SCONV), F32),
        pltpu.VMEM((halo[2] + tm, D_CCONV), F32),
        pltpu.VMEM((tm, D_CCONV), F32),
    ]
    body = functools.partial(_mixer_body, tm=tm, tstride=tstride,
                             tiles_per_seq=tiles_per_seq, has_state=has_state)
    return pl.pallas_call(
        body,
        grid=(n_tiles,),
        in_specs=in_specs,
        out_specs=out_specs,
        out_shape=out_shape,
        scratch_shapes=scratch,
        compiler_params=pltpu.CompilerParams(
            dimension_semantics=("arbitrary",), vmem_limit_bytes=VMEM_LIMIT_BYTES),
        name="mixer_state" if has_state else "mixer_prompt",
    )(*args)


def _ffn_body(*refs, tm, tstride, tiles_per_seq, has_state, final_norm):
    it = iter(refs)
    x_ref = next(it)
    if has_state:
        st_a_ref, st_g_ref = next(it), next(it)
    g2_ref, wa_ref, wg_ref, cwa_ref, cwg_ref, wd_ref = (
        next(it), next(it), next(it), next(it), next(it), next(it))
    if final_norm:
        gf_ref = next(it)
    o_ref, new_a_ref, new_g_ref = next(it), next(it), next(it)
    h_ref, ext_ref = next(it), next(it)
    if not has_state:
        halo_ref = next(it)

    s = tstride
    ck = wa_ref.shape[1]
    hl = ext_ref.shape[0] - tm
    i = pl.program_id(0)
    j = pl.program_id(1)
    nj = pl.num_programs(1)

    @pl.when(j == 0)
    def _():
        x = _rows(x_ref)
        h_ref[...] = (x * _rms_scale(x) * g2_ref[...]).astype(BF16)
        _store_rows(o_ref, x)
        if not has_state:
            @pl.when(i % tiles_per_seq == 0)
            def _():
                halo_ref[...] = jnp.zeros(halo_ref.shape, F32)

    if has_state:
        ext_ref[0:hl, 0:ck] = _rows(st_a_ref)
        ext_ref[0:hl, ck:2 * ck] = _rows(st_g_ref)
    else:
        ext_ref[0:hl, :] = halo_ref[j]

    up_a = _dot(h_ref[...], wa_ref[...].astype(BF16))
    up_g = _dot(h_ref[...], wg_ref[...].astype(BF16))
    ext_ref[hl:hl + tm, 0:ck] = up_a
    ext_ref[hl:hl + tm, ck:2 * ck] = up_g

    def conv(up, cw_ref, lo):
        out = cw_ref[FFN_CONV_K - 1:FFN_CONV_K, :] * up
        for t in range(FFN_CONV_K - 1):
            back = (FFN_CONV_K - 1 - t) * s
            out = out + cw_ref[t:t + 1, :] * ext_ref[hl - back:hl - back + tm, lo:lo + ck]
        return out

    c_a = conv(up_a, cwa_ref, 0)
    c_g = conv(up_g, cwg_ref, ck)
    act = (c_a * jax.nn.sigmoid(c_a) * c_g).astype(BF16)
    o_ref[...] += _dot(act, wd_ref[...].astype(BF16)).reshape(o_ref.shape)

    nb = (FFN_CONV_K - 1) * s
    new_a = ext_ref[hl + tm - nb:hl + tm, 0:ck]
    new_g = ext_ref[hl + tm - nb:hl + tm, ck:2 * ck]
    if has_state:
        _store_rows(new_a_ref, new_a)
        _store_rows(new_g_ref, new_g)
    else:
        seq = i // tiles_per_seq
        new_a_ref[seq, j] = new_a
        new_g_ref[seq, j] = new_g
        halo_ref[j] = ext_ref[tm:tm + hl, :]

    if final_norm:
        @pl.when(j == nj - 1)
        def _():
            y = _rows(o_ref)
            _store_rows(o_ref, y * _rms_scale(y) * gf_ref[...])


def _ffn_call(x, state, w, layer, final_g, *, tm, tstride, tiles_per_seq, n_seq):
    has_state = state is not None
    final_norm = final_g is not None
    ck = FFN_CK
    nj = D_FF // ck
    nb = FFN_CONV_K - 1
    if has_state:
        t_steps, n_b, _ = x.shape
        n_tiles = n_b // tstride
        hl = nb * tstride
        x_spec = pl.BlockSpec((t_steps, tstride, D_MODEL), lambda i, j: (0, i, 0))
    else:
        n_tiles = x.shape[0] // tm
        hl = _round_up(nb, SUBLANES)
        x_spec = pl.BlockSpec((tm, D_MODEL), lambda i, j: (i, 0))

    in_specs = [x_spec]
    args = [x]
    if has_state:
        in_specs += [pl.BlockSpec((None, nb, tstride, ck), lambda i, j: (layer, 0, i, j)),
                     pl.BlockSpec((None, nb, tstride, ck), lambda i, j: (layer, 0, i, nj + j))]
        args += [state, state]
    in_specs += [
        _layer_spec((1, D_MODEL), layer),
        pl.BlockSpec((None, D_MODEL, ck), lambda i, j: (layer, 0, j)),
        pl.BlockSpec((None, D_MODEL, ck), lambda i, j: (layer, 0, nj + j)),
        pl.BlockSpec((None, FFN_CONV_K, ck), lambda i, j: (layer, 0, j)),
        pl.BlockSpec((None, FFN_CONV_K, ck), lambda i, j: (layer, 0, nj + j)),
        pl.BlockSpec((None, ck, D_MODEL), lambda i, j: (layer, j, 0)),
    ]
    args += [w["norm2_g"], w["w_up"], w["w_up"], w["ffn_conv_w"], w["ffn_conv_w"], w["w_down"]]
    if final_norm:
        in_specs.append(pl.BlockSpec((1, D_MODEL), lambda i, j: (0, 0)))
        args.append(final_g)

    out_shape = [jax.ShapeDtypeStruct(x.shape, F32)]
    out_specs = [x_spec]
    for _ in range(2):
        if has_state:
            out_shape.append(jax.ShapeDtypeStruct((nb, n_b, D_FF), F32))
            out_specs.append(pl.BlockSpec((nb, tstride, ck), lambda i, j: (0, i, j)))
        else:
            out_shape.append(jax.ShapeDtypeStruct((n_seq, nj, nb, ck), F32))
            out_specs.append(pl.BlockSpec((n_seq, nj, nb, ck), lambda i, j: (0, 0, 0, 0)))

    scratch = [pltpu.VMEM((tm, D_MODEL), BF16), pltpu.VMEM((hl + tm, 2 * ck), F32)]
    if not has_state:
        scratch.append(pltpu.VMEM((nj, hl, 2 * ck), F32))
    body = functools.partial(_ffn_body, tm=tm, tstride=tstride, tiles_per_seq=tiles_per_seq,
                             has_state=has_state, final_norm=final_norm)
    return pl.pallas_call(
        body,
        grid=(n_tiles, nj),
        in_specs=in_specs,
        out_specs=out_specs,
        out_shape=out_shape,
        scratch_shapes=scratch,
        compiler_params=pltpu.CompilerParams(
            dimension_semantics=("arbitrary", "arbitrary"), vmem_limit_bytes=VMEM_LIMIT_BYTES),
        name="ffn_state" if has_state else "ffn_prompt",
    )(*args)


def _time_major(a):
    return jnp.swapaxes(a, -3, -2)


def kernel(x_prompt, x_sample, state_pool, state_sconv, state_cconv, state_ffn, norm1_g, w_in,
           pool_w, pool_scale, sconv_w, cconv_w, cconv_b, cconv_norm_g, w_out, norm2_g, w_up,
           ffn_conv_w, w_down, final_norm_g):
    batch, seq, _ = x_prompt.shape
    dec_batch, dec_seq, _ = x_sample.shape
    depth = w_in.shape[0]

    w = {
        "norm1_g": norm1_g.reshape(depth, 1, D_MODEL),
        "w_in": w_in.astype(BF16),
        "pool_w": pool_w.astype(BF16),
        "pool_scale": pool_scale.reshape(depth, 1, D_POOL),
        "sconv_w": sconv_w,
        "cconv_w": cconv_w,
        "cconv_b": cconv_b.reshape(depth, 1, D_CCONV),
        "cconv_norm_g": cconv_norm_g.reshape(depth, 1, D_CCONV),
        "w_out": w_out.astype(BF16),
        "norm2_g": norm2_g.reshape(depth, 1, D_MODEL),
        "w_up": w_up,
        "ffn_conv_w": ffn_conv_w,
        "w_down": w_down,
    }
    final_g = final_norm_g.reshape(1, D_MODEL)

    xp = x_prompt.reshape(batch * seq, D_MODEL)
    xs = _time_major(x_sample)
    st_mix = [_time_major(state_pool), _time_major(state_sconv), _time_major(state_cconv)]
    st_ffn = _time_major(state_ffn)

    outs_p = [[], [], [], []]
    outs_s = [[], [], [], []]
    for l in range(depth):
        last = final_g if l == depth - 1 else None

        xp, pool_p, sconv_p, cconv_p = _mixer_call(
            xp, None, w, l, tm=MIX_TM_PROMPT, tstride=1,
            tiles_per_seq=seq // MIX_TM_PROMPT, n_seq=batch)
        xp, ffa_p, ffg_p = _ffn_call(
            xp, None, w, l, last, tm=FFN_TM_PROMPT, tstride=1,
            tiles_per_seq=seq // FFN_TM_PROMPT, n_seq=batch)
        unchunk = lambda a: a.transpose(0, 2, 1, 3).reshape(batch, FFN_CONV_K - 1, D_FF)
        outs_p[0].append(pool_p)
        outs_p[1].append(sconv_p)
        outs_p[2].append(cconv_p)
        outs_p[3].append(jnp.concatenate([unchunk(ffa_p), unchunk(ffg_p)], axis=-1))

        xs, pool_s, sconv_s, cconv_s = _mixer_call(
            xs, st_mix, w, l, tm=dec_seq * MIX_SAMPLE_SEQS, tstride=MIX_SAMPLE_SEQS,
            tiles_per_seq=1, n_seq=dec_batch)
        xs, ffa_s, ffg_s = _ffn_call(
            xs, st_ffn, w, l, last, tm=dec_seq * dec_batch, tstride=dec_batch,
            tiles_per_seq=1, n_seq=dec_batch)
        outs_s[0].append(pool_s)
        outs_s[1].append(sconv_s)
        outs_s[2].append(cconv_s)
        outs_s[3].append(jnp.concatenate([ffa_s, ffg_s], axis=-1))

    y_prompt = xp.reshape(batch, seq, D_MODEL)
    y_sample = _time_major(xs)
    stack_tm = lambda parts: _time_major(jnp.stack(parts))
    return (y_prompt, y_sample,
            jnp.stack(outs_p[0]), jnp.stack(outs_p[1]), jnp.stack(outs_p[2]), jnp.stack(outs_p[3]),
            stack_tm(outs_s[0]), stack_tm(outs_s[1]), stack_tm(outs_s[2]), stack_tm(outs_s[3]))
```

```python
import functools

import jax
import jax.numpy as jnp
from jax import lax
from jax.experimental import pallas as pl
from jax.experimental.pallas import tpu as pltpu

EPS = 1e-6
D_MODEL = 2048
D_POOL = 512
D_SCONV = 768
D_CCONV = 768
POOL_WINDOWS = (2, 4, 8, 16)
POOL_GROUP = 128
POOL_BUF = 15
SCONV_K = 3
CCONV_K = 31
FFN_CONV_K = 3
D_FF = 5632
D_IN = D_POOL + 3 * D_SCONV + 2 * D_CCONV
O1 = D_POOL
O2 = O1 + D_SCONV
O3 = O2 + D_SCONV
O4 = O3 + D_SCONV
O5 = O4 + D_CCONV

SUBLANES = 8
LANES = 128
MXU_N = 256
VMEM_LIMIT_BYTES = 56 * 1024 * 1024

MIX_TM_PROMPT = 256
MIX_SAMPLE_SEQS = 32
FFN_TM_PROMPT = 1024
FFN_CK = 256
FFN_ROW_SPLITS = 4
CONV_ROW_BLOCK = 32

BF16 = jnp.bfloat16
F32 = jnp.float32


def _round_up(n, m):
    return (n + m - 1) // m * m


def _dot(a, b):
    return jnp.dot(a, b, preferred_element_type=F32)


def _rms_scale(x):
    return lax.rsqrt(jnp.mean(x * x, axis=-1, keepdims=True) + EPS)


def _rows(ref):
    v = ref[...]
    return v.reshape(-1, v.shape[-1])


def _store_rows(ref, v):
    ref[...] = v.reshape(ref.shape)


def _conv_aligned_stride(ext_ref, w_ref, b_ref, out_ref, ln, halo, tm, tstride):
    k = w_ref.shape[0]
    for rb in range(tm // CONV_ROW_BLOCK):
        r0 = rb * CONV_ROW_BLOCK
        acc = jnp.broadcast_to(b_ref[:, ln], (CONV_ROW_BLOCK, LANES))
        for t in range(k):
            start = halo - (k - 1 - t) * tstride + r0
            acc = acc + w_ref[t:t + 1, ln] * ext_ref[start:start + CONV_ROW_BLOCK, ln]
        out_ref[r0:r0 + CONV_ROW_BLOCK, ln] = acc


def _shift_rows(ext_ref, shift_ref, ln):
    row = lax.broadcasted_iota(jnp.int32, (SUBLANES, LANES), 0)
    prev = None
    for q in range(ext_ref.shape[0] // SUBLANES):
        t = ext_ref[q * SUBLANES:(q + 1) * SUBLANES, ln]
        rolled = [pltpu.roll(t, r, axis=0) for r in range(1, SUBLANES)]
        if prev is not None:
            for r in range(1, SUBLANES):
                shift_ref[r - 1, q * SUBLANES:(q + 1) * SUBLANES, ln] = jnp.where(
                    row >= r, rolled[r - 1], prev[r - 1])
        prev = rolled


def _conv_unit_stride(ext_ref, shift_ref, w_ref, b_ref, out_ref, ln, halo, tm):
    k = w_ref.shape[0]
    for rb in range(tm // CONV_ROW_BLOCK):
        r0 = rb * CONV_ROW_BLOCK
        acc = jnp.broadcast_to(b_ref[:, ln], (CONV_ROW_BLOCK, LANES))
        for d in range(k):
            a, r = divmod(d, SUBLANES)
            start = halo + r0 - a * SUBLANES
            src = ext_ref if r == 0 else shift_ref.at[r - 1]
            acc = acc + w_ref[k - 1 - d:k - d, ln] * src[start:start + CONV_ROW_BLOCK, ln]
        out_ref[r0:r0 + CONV_ROW_BLOCK, ln] = acc


def _mixer_body(*refs, tm, tstride, tiles_per_seq, has_state):
    it = iter(refs)
    x_ref = next(it)
    if has_state:
        st_pool_ref, st_sconv_ref, st_cconv_ref = next(it), next(it), next(it)
    g1_ref, w_in_ref, pool_w_ref, pool_scale_ref = next(it), next(it), next(it), next(it)
    sconv_w_ref, cconv_w_ref, cconv_b_ref, cconv_g_ref, w_out_ref = (
        next(it), next(it), next(it), next(it), next(it))
    o_ref, pool_o_ref, sconv_o_ref, cconv_o_ref = next(it), next(it), next(it), next(it)
    h_ref, mix_ref, ext_p_ref, ext_s_ref, ext_c_ref, cc_ref, proj_s_ref = (
        next(it), next(it), next(it), next(it), next(it), next(it), next(it))
    if not has_state:
        shift_ref = next(it)

    s = tstride
    hp = ext_p_ref.shape[0] - tm
    hs = ext_s_ref.shape[0] - tm
    hc = ext_c_ref.shape[0] - tm
    i = pl.program_id(0)

    if has_state:
        ext_p_ref[0:hp, :] = _rows(st_pool_ref)
        ext_s_ref[0:hs, :] = _rows(st_sconv_ref)
        ext_c_ref[0:hc, :] = _rows(st_cconv_ref)
    else:
        @pl.when(i % tiles_per_seq == 0)
        def _():
            ext_p_ref[0:hp, :] = jnp.zeros((hp, D_POOL), F32)
            ext_s_ref[0:hs, :] = jnp.zeros((hs, D_SCONV), F32)
            ext_c_ref[0:hc, :] = jnp.zeros((hc, D_CCONV), F32)

    x = _rows(x_ref)
    h_ref[...] = (x * _rms_scale(x) * g1_ref[...]).astype(BF16)

    def in_proj(col0, c):
        return _dot(h_ref[...], w_in_ref[:, col0 + c * MXU_N:col0 + (c + 1) * MXU_N])

    for c in range(D_CCONV // MXU_N):
        ln = slice(c * MXU_N, (c + 1) * MXU_N)
        ext_c_ref[hc:hc + tm, ln] = in_proj(O4, c) * jax.nn.sigmoid(in_proj(O5, c))

    for c in range(3 * D_SCONV // MXU_N):
        proj_s_ref[:, c * MXU_N:(c + 1) * MXU_N] = in_proj(O1, c)
    for c in range(D_POOL // MXU_N):
        ext_p_ref[hp:hp + tm, c * MXU_N:(c + 1) * MXU_N] = in_proj(0, c)

    for lt in range(D_CCONV // LANES):
        ln = slice(lt * LANES, (lt + 1) * LANES)
        if s == 1:
            _shift_rows(ext_c_ref, shift_ref, ln)
            _conv_unit_stride(ext_c_ref, shift_ref, cconv_w_ref, cconv_b_ref, cc_ref, ln, hc, tm)
        else:
            _conv_aligned_stride(ext_c_ref, cconv_w_ref, cconv_b_ref, cc_ref, ln, hc, tm, s)
    cc = cc_ref[...]
    z = cc * _rms_scale(cc) * cconv_g_ref[...]
    mix_ref[:, O2:D_MODEL] = (z * jax.nn.sigmoid(z)).astype(BF16)

    gate_b = proj_s_ref[:, 0:D_SCONV]
    q = proj_s_ref[:, D_SCONV:2 * D_SCONV] * proj_s_ref[:, 2 * D_SCONV:3 * D_SCONV]
    ext_s_ref[hs:hs + tm, :] = q
    conv = sconv_w_ref[SCONV_K - 1:SCONV_K, :] * q
    for t in range(SCONV_K - 1):
        back = (SCONV_K - 1 - t) * s
        conv = conv + sconv_w_ref[t:t + 1, :] * ext_s_ref[hs - back:hs - back + tm, :]
    mix_ref[:, O1:O2] = (gate_b * conv).astype(BF16)

    if not has_state:
        pos = (i % tiles_per_seq) * tm + lax.broadcasted_iota(jnp.int32, (tm, 1), 0)
    for g, k in enumerate(POOL_WINDOWS):
        ln = slice(g * POOL_GROUP, (g + 1) * POOL_GROUP)
        u = ext_p_ref[hp:hp + tm, ln]
        win = u
        for m in range(1, k):
            win = win + ext_p_ref[hp - m * s:hp - m * s + tm, ln]
        if has_state:
            pooled = win / float(k)
        else:
            pooled = win / jnp.minimum(pos + 1, k).astype(F32)
        d = (pooled - u).astype(BF16)
        y = _dot(d, pool_w_ref[g]) * pool_scale_ref[:, ln]
        mix_ref[:, ln] = y.astype(BF16)

    _store_rows(o_ref, _rows(x_ref) + _dot(mix_ref[...], w_out_ref[...]))

    nb_p, nb_s, nb_c = POOL_BUF * s, (SCONV_K - 1) * s, (CCONV_K - 1) * s
    new_p = ext_p_ref[hp + tm - nb_p:hp + tm, :]
    new_s = ext_s_ref[hs + tm - nb_s:hs + tm, :]
    new_c = ext_c_ref[hc + tm - nb_c:hc + tm, :]
    if has_state:
        _store_rows(pool_o_ref, new_p)
        _store_rows(sconv_o_ref, new_s)
        _store_rows(cconv_o_ref, new_c)
    else:
        pool_o_ref[0] = new_p
        sconv_o_ref[0] = new_s
        cconv_o_ref[0] = new_c
        ext_p_ref[0:hp, :] = ext_p_ref[tm:tm + hp, :]
        ext_s_ref[0:hs, :] = ext_s_ref[tm:tm + hs, :]
        ext_c_ref[0:hc, :] = ext_c_ref[tm:tm + hc, :]


def _layer_spec(shape, layer, single_buffer=False):
    idx = lambda *_: (layer,) + (0,) * len(shape)
    if single_buffer:
        return pl.BlockSpec((None,) + shape, idx, pipeline_mode=pl.Buffered(1))
    return pl.BlockSpec((None,) + shape, idx)


def _mixer_call(x, states, w, layer, *, tm, tstride, tiles_per_seq, n_seq):
    has_state = states is not None
    widths = [D_POOL, D_SCONV, D_CCONV]
    bufs = [POOL_BUF, SCONV_K - 1, CCONV_K - 1]
    if has_state:
        t_steps, n_b, _ = x.shape
        n_tiles = n_b // tstride
        halo = [nb * tstride for nb in bufs]
        x_spec = pl.BlockSpec((t_steps, tstride, D_MODEL), lambda i: (0, i, 0))
    else:
        n_tiles = x.shape[0] // tm
        halo = [_round_up(nb, SUBLANES) for nb in bufs]
        x_spec = pl.BlockSpec((tm, D_MODEL), lambda i: (i, 0))

    in_specs = [x_spec]
    args = [x]
    if has_state:
        for st, nb, wd in zip(states, bufs, widths):
            in_specs.append(pl.BlockSpec((None, nb, tstride, wd), lambda i: (layer, 0, i, 0),
                                         pipeline_mode=pl.Buffered(1)))
            args.append(st)
    in_specs += [
        _layer_spec((1, D_MODEL), layer),
        _layer_spec((D_MODEL, D_IN), layer, single_buffer=True),
        _layer_spec((len(POOL_WINDOWS), POOL_GROUP, POOL_GROUP), layer),
        _layer_spec((1, D_POOL), layer),
        _layer_spec((SCONV_K, D_SCONV), layer),
        _layer_spec((CCONV_K, D_CCONV), layer),
        _layer_spec((1, D_CCONV), layer),
        _layer_spec((1, D_CCONV), layer),
        _layer_spec((D_MODEL, D_MODEL), layer, single_buffer=True),
    ]
    args += [w["norm1_g"], w["w_in"], w["pool_w"], w["pool_scale"], w["sconv_w"],
             w["cconv_w"], w["cconv_b"], w["cconv_norm_g"], w["w_out"]]

    out_shape = [jax.ShapeDtypeStruct(x.shape, F32)]
    out_specs = [x_spec]
    for nb, wd in zip(bufs, widths):
        if has_state:
            out_shape.append(jax.ShapeDtypeStruct((nb, n_b, wd), F32))
            out_specs.append(pl.BlockSpec((nb, tstride, wd), lambda i: (0, i, 0)))
        else:
            out_shape.append(jax.ShapeDtypeStruct((n_seq, nb, wd), F32))
            out_specs.append(pl.BlockSpec((1, nb, wd), lambda i: (i // tiles_per_seq, 0, 0)))

    scratch = [
        pltpu.VMEM((tm, D_MODEL), BF16),
        pltpu.VMEM((tm, D_MODEL), BF16),
        pltpu.VMEM((halo[0] + tm, D_POOL), F32),
        pltpu.VMEM((halo[1] + tm, D_SCONV), F32),
        pltpu.VMEM((halo[2] + tm, D_CCONV), F32),
        pltpu.VMEM((tm, D_CCONV), F32),
        pltpu.VMEM((tm, 3 * D_SCONV), F32),
    ]
    if not has_state:
        scratch.append(pltpu.VMEM((SUBLANES - 1, halo[2] + tm, D_CCONV), F32))
    body = functools.partial(_mixer_body, tm=tm, tstride=tstride,
                             tiles_per_seq=tiles_per_seq, has_state=has_state)
    return pl.pallas_call(
        body,
        grid=(n_tiles,),
        in_specs=in_specs,
        out_specs=out_specs,
        out_shape=out_shape,
        scratch_shapes=scratch,
        compiler_params=pltpu.CompilerParams(
            dimension_semantics=("arbitrary",), vmem_limit_bytes=VMEM_LIMIT_BYTES),
        name="mixer_state" if has_state else "mixer_prompt",
    )(*args)


def _row_block(ref, r0, rows):
    if len(ref.shape) == 2:
        return ref.at[r0:r0 + rows]
    n_b = ref.shape[1]
    return ref.at[r0 // n_b:(r0 + rows) // n_b]


def _ffn_body(*refs, tm, tstride, tiles_per_seq, has_state, final_norm):
    it = iter(refs)
    x_ref = next(it)
    if has_state:
        st_a_ref, st_g_ref = next(it), next(it)
    g2_ref, wa_ref, wg_ref, cwa_ref, cwg_ref, wd_ref = (
        next(it), next(it), next(it), next(it), next(it), next(it))
    if final_norm:
        gf_ref = next(it)
    o_ref, new_a_ref, new_g_ref = next(it), next(it), next(it)
    h_ref, ext_ref = next(it), next(it)
    if not has_state:
        halo_ref = next(it)

    s = tstride
    ck = wa_ref.shape[1]
    hl = ext_ref.shape[0] - tm
    rows = tm // FFN_ROW_SPLITS
    i = pl.program_id(0)
    j = pl.program_id(1)
    nj = pl.num_programs(1)

    @pl.when(j == 0)
    def _():
        x = _rows(x_ref)
        h_ref[...] = (x * _rms_scale(x) * g2_ref[...]).astype(BF16)
        _store_rows(o_ref, x)
        if not has_state:
            @pl.when(i % tiles_per_seq == 0)
            def _():
                halo_ref[...] = jnp.zeros(halo_ref.shape, F32)

    if has_state:
        ext_ref[0:hl, 0:ck] = _rows(st_a_ref)
        ext_ref[0:hl, ck:2 * ck] = _rows(st_g_ref)
    else:
        ext_ref[0:hl, :] = halo_ref[j]

    w_a = wa_ref[...].astype(BF16)
    w_g = wg_ref[...].astype(BF16)
    w_d = wd_ref[...].astype(BF16)
    for r in range(FFN_ROW_SPLITS):
        r0 = r * rows
        h = h_ref[r0:r0 + rows, :]
        ext_ref[hl + r0:hl + r0 + rows, 0:ck] = _dot(h, w_a)
        ext_ref[hl + r0:hl + r0 + rows, ck:2 * ck] = _dot(h, w_g)

    def conv(cw_ref, r0, lo):
        out = None
        for t in range(FFN_CONV_K):
            start = hl + r0 - (FFN_CONV_K - 1 - t) * s
            term = cw_ref[t:t + 1, :] * ext_ref[start:start + rows, lo:lo + ck]
            out = term if out is None else out + term
        return out

    for r in range(FFN_ROW_SPLITS):
        r0 = r * rows
        c_a = conv(cwa_ref, r0, 0)
        c_g = conv(cwg_ref, r0, ck)
        act = (c_a * jax.nn.sigmoid(c_a) * c_g).astype(BF16)
        o_blk = _row_block(o_ref, r0, rows)
        _store_rows(o_blk, _rows(o_blk) + _dot(act, w_d))

    nb = (FFN_CONV_K - 1) * s
    new_a = ext_ref[hl + tm - nb:hl + tm, 0:ck]
    new_g = ext_ref[hl + tm - nb:hl + tm, ck:2 * ck]
    if has_state:
        _store_rows(new_a_ref, new_a)
        _store_rows(new_g_ref, new_g)
    else:
        seq = i // tiles_per_seq
        new_a_ref[seq, j] = new_a
        new_g_ref[seq, j] = new_g
        halo_ref[j] = ext_ref[tm:tm + hl, :]

    if final_norm:
        @pl.when(j == nj - 1)
        def _():
            y = _rows(o_ref)
            _store_rows(o_ref, y * _rms_scale(y) * gf_ref[...])


def _ffn_call(x, state, w, layer, final_g, *, tm, tstride, tiles_per_seq, n_seq):
    has_state = state is not None
    final_norm = final_g is not None
    ck = FFN_CK
    nj = D_FF // ck
    nb = FFN_CONV_K - 1
    if has_state:
        t_steps, n_b, _ = x.shape
        n_tiles = n_b // tstride
        hl = nb * tstride
        x_block, x_index = (t_steps, tstride, D_MODEL), lambda i, j: (0, i, 0)
    else:
        n_tiles = x.shape[0] // tm
        hl = _round_up(nb, SUBLANES)
        x_block, x_index = (tm, D_MODEL), lambda i, j: (i, 0)

    in_specs = [pl.BlockSpec(x_block, x_index, pipeline_mode=pl.Buffered(1))]
    args = [x]
    if has_state:
        in_specs += [pl.BlockSpec((None, nb, tstride, ck), lambda i, j: (layer, 0, i, j)),
                     pl.BlockSpec((None, nb, tstride, ck), lambda i, j: (layer, 0, i, nj + j))]
        args += [state, state]
    in_specs += [
        _layer_spec((1, D_MODEL), layer),
        pl.BlockSpec((None, D_MODEL, ck), lambda i, j: (layer, 0, j)),
        pl.BlockSpec((None, D_MODEL, ck), lambda i, j: (layer, 0, nj + j)),
        pl.BlockSpec((None, FFN_CONV_K, ck), lambda i, j: (layer, 0, j)),
        pl.BlockSpec((None, FFN_CONV_K, ck), lambda i, j: (layer, 0, nj + j)),
        pl.BlockSpec((None, ck, D_MODEL), lambda i, j: (layer, j, 0)),
    ]
    args += [w["norm2_g"], w["w_up"], w["w_up"], w["ffn_conv_w"], w["ffn_conv_w"], w["w_down"]]
    if final_norm:
        in_specs.append(pl.BlockSpec((1, D_MODEL), lambda i, j: (0, 0)))
        args.append(final_g)

    out_shape = [jax.ShapeDtypeStruct(x.shape, F32)]
    out_specs = [pl.BlockSpec(x_block, x_index)]
    for _ in range(2):
        if has_state:
            out_shape.append(jax.ShapeDtypeStruct((nb, n_b, D_FF), F32))
            out_specs.append(pl.BlockSpec((nb, tstride, ck), lambda i, j: (0, i, j)))
        else:
            out_shape.append(jax.ShapeDtypeStruct((n_seq, nj, nb, ck), F32))
            out_specs.append(pl.BlockSpec((n_seq, nj, nb, ck), lambda i, j: (0, 0, 0, 0)))

    scratch = [pltpu.VMEM((tm, D_MODEL), BF16), pltpu.VMEM((hl + tm, 2 * ck), F32)]
    if not has_state:
        scratch.append(pltpu.VMEM((nj, hl, 2 * ck), F32))
    body = functools.partial(_ffn_body, tm=tm, tstride=tstride, tiles_per_seq=tiles_per_seq,
                             has_state=has_state, final_norm=final_norm)
    return pl.pallas_call(
        body,
        grid=(n_tiles, nj),
        in_specs=in_specs,
        out_specs=out_specs,
        out_shape=out_shape,
        scratch_shapes=scratch,
        compiler_params=pltpu.CompilerParams(
            dimension_semantics=("arbitrary", "arbitrary"), vmem_limit_bytes=VMEM_LIMIT_BYTES),
        name="ffn_state" if has_state else "ffn_prompt",
    )(*args)


def _time_major(a):
    return jnp.swapaxes(a, -3, -2)


def kernel(x_prompt, x_sample, state_pool, state_sconv, state_cconv, state_ffn, norm1_g, w_in,
           pool_w, pool_scale, sconv_w, cconv_w, cconv_b, cconv_norm_g, w_out, norm2_g, w_up,
           ffn_conv_w, w_down, final_norm_g):
    batch, seq, _ = x_prompt.shape
    dec_batch, dec_seq, _ = x_sample.shape
    depth = w_in.shape[0]

    w = {
        "norm1_g": norm1_g.reshape(depth, 1, D_MODEL),
        "w_in": w_in.astype(BF16),
        "pool_w": pool_w.astype(BF16),
        "pool_scale": pool_scale.reshape(depth, 1, D_POOL),
        "sconv_w": sconv_w,
        "cconv_w": cconv_w,
        "cconv_b": cconv_b.reshape(depth, 1, D_CCONV),
        "cconv_norm_g": cconv_norm_g.reshape(depth, 1, D_CCONV),
        "w_out": w_out.astype(BF16),
        "norm2_g": norm2_g.reshape(depth, 1, D_MODEL),
        "w_up": w_up,
        "ffn_conv_w": ffn_conv_w,
        "w_down": w_down,
    }
    final_g = final_norm_g.reshape(1, D_MODEL)

    xp = x_prompt.reshape(batch * seq, D_MODEL)
    xs = _time_major(x_sample)
    st_mix = [_time_major(state_pool), _time_major(state_sconv), _time_major(state_cconv)]
    st_ffn = _time_major(state_ffn)

    outs_p = [[], [], [], []]
    outs_s = [[], [], [], []]
    for l in range(depth):
        last = final_g if l == depth - 1 else None

        xp, pool_p, sconv_p, cconv_p = _mixer_call(
            xp, None, w, l, tm=MIX_TM_PROMPT, tstride=1,
            tiles_per_seq=seq // MIX_TM_PROMPT, n_seq=batch)
        xp, ffa_p, ffg_p = _ffn_call(
            xp, None, w, l, last, tm=FFN_TM_PROMPT, tstride=1,
            tiles_per_seq=seq // FFN_TM_PROMPT, n_seq=batch)
        unchunk = lambda a: a.transpose(0, 2, 1, 3).reshape(batch, FFN_CONV_K - 1, D_FF)
        outs_p[0].append(pool_p)
        outs_p[1].append(sconv_p)
        outs_p[2].append(cconv_p)
        outs_p[3].append(jnp.concatenate([unchunk(ffa_p), unchunk(ffg_p)], axis=-1))

        xs, pool_s, sconv_s, cconv_s = _mixer_call(
            xs, st_mix, w, l, tm=dec_seq * MIX_SAMPLE_SEQS, tstride=MIX_SAMPLE_SEQS,
            tiles_per_seq=1, n_seq=dec_batch)
        xs, ffa_s, ffg_s = _ffn_call(
            xs, st_ffn, w, l, last, tm=dec_seq * dec_batch, tstride=dec_batch,
            tiles_per_seq=1, n_seq=dec_batch)
        outs_s[0].append(pool_s)
        outs_s[1].append(sconv_s)
        outs_s[2].append(cconv_s)
        outs_s[3].append(jnp.concatenate([ffa_s, ffg_s], axis=-1))

    y_prompt = xp.reshape(batch, seq, D_MODEL)
    y_sample = _time_major(xs)
    stack_tm = lambda parts: _time_major(jnp.stack(parts))
    return (y_prompt, y_sample,
            jnp.stack(outs_p[0]), jnp.stack(outs_p[1]), jnp.stack(outs_p[2]), jnp.stack(outs_p[3]),
            stack_tm(outs_s[0]), stack_tm(outs_s[1]), stack_tm(outs_s[2]), stack_tm(outs_s[3]))
```

```python
import functools

import jax
import jax.numpy as jnp
from jax import lax
from jax.experimental import pallas as pl
from jax.experimental.pallas import tpu as pltpu

EPS = 1e-6
D_MODEL = 2048
D_POOL = 512
D_SCONV = 768
D_CCONV = 768
POOL_WINDOWS = (2, 4, 8, 16)
POOL_GROUP = 128
POOL_BUF = 15
SCONV_K = 3
CCONV_K = 31
FFN_CONV_K = 3
D_FF = 5632
D_IN = D_POOL + 3 * D_SCONV + 2 * D_CCONV
O1 = D_POOL
O2 = O1 + D_SCONV
O3 = O2 + D_SCONV
O4 = O3 + D_SCONV
O5 = O4 + D_CCONV

SUBLANES = 8
LANES = 128
MXU_N = 256
VMEM_LIMIT_BYTES = 56 * 1024 * 1024

MIX_TM_PROMPT = 256
MIX_SAMPLE_SEQS = 32
FFN_TM_PROMPT = 1024
FFN_CK = 512
FFN_ROW_SPLITS = 4
CONV_ROW_BLOCK = 32

BF16 = jnp.bfloat16
F32 = jnp.float32


def _round_up(n, m):
    return (n + m - 1) // m * m


def _dot(a, b):
    return jnp.dot(a, b, preferred_element_type=F32)


def _rms_scale(x):
    return lax.rsqrt(jnp.mean(x * x, axis=-1, keepdims=True) + EPS)


def _rows(ref):
    v = ref[...]
    return v.reshape(-1, v.shape[-1])


def _store_rows(ref, v):
    ref[...] = v.reshape(ref.shape)


def _conv_aligned_stride(ext_ref, w_ref, b_ref, out_ref, ln, halo, tm, tstride):
    k = w_ref.shape[0]
    for rb in range(tm // CONV_ROW_BLOCK):
        r0 = rb * CONV_ROW_BLOCK
        acc = jnp.broadcast_to(b_ref[:, ln], (CONV_ROW_BLOCK, LANES))
        for t in range(k):
            start = halo - (k - 1 - t) * tstride + r0
            acc = acc + w_ref[t:t + 1, ln] * ext_ref[start:start + CONV_ROW_BLOCK, ln]
        out_ref[r0:r0 + CONV_ROW_BLOCK, ln] = acc


def _shift_rows(ext_ref, shift_ref, ln):
    row = lax.broadcasted_iota(jnp.int32, (SUBLANES, LANES), 0)
    prev = None
    for q in range(ext_ref.shape[0] // SUBLANES):
        t = ext_ref[q * SUBLANES:(q + 1) * SUBLANES, ln]
        rolled = [pltpu.roll(t, r, axis=0) for r in range(1, SUBLANES)]
        if prev is not None:
            for r in range(1, SUBLANES):
                shift_ref[r - 1, q * SUBLANES:(q + 1) * SUBLANES, ln] = jnp.where(
                    row >= r, rolled[r - 1], prev[r - 1])
        prev = rolled


def _conv_unit_stride(ext_ref, shift_ref, w_ref, b_ref, out_ref, ln, halo, tm):
    k = w_ref.shape[0]
    for rb in range(tm // CONV_ROW_BLOCK):
        r0 = rb * CONV_ROW_BLOCK
        acc = jnp.broadcast_to(b_ref[:, ln], (CONV_ROW_BLOCK, LANES))
        for d in range(k):
            a, r = divmod(d, SUBLANES)
            start = halo + r0 - a * SUBLANES
            src = ext_ref if r == 0 else shift_ref.at[r - 1]
            acc = acc + w_ref[k - 1 - d:k - d, ln] * src[start:start + CONV_ROW_BLOCK, ln]
        out_ref[r0:r0 + CONV_ROW_BLOCK, ln] = acc


def _mixer_body(*refs, tm, tstride, tiles_per_seq, has_state, round_ffn_weights):
    it = iter(refs)
    x_ref = next(it)
    if has_state:
        st_pool_ref, st_sconv_ref, st_cconv_ref = next(it), next(it), next(it)
    g1_ref, w_in_ref, pool_w_ref, pool_scale_ref = next(it), next(it), next(it), next(it)
    sconv_w_ref, cconv_w_ref, cconv_b_ref, cconv_g_ref, w_out_ref = (
        next(it), next(it), next(it), next(it), next(it))
    if round_ffn_weights:
        w_up_ref, w_down_ref = next(it), next(it)
    o_ref, pool_o_ref, sconv_o_ref, cconv_o_ref = next(it), next(it), next(it), next(it)
    if round_ffn_weights:
        w_up_o_ref, w_down_o_ref = next(it), next(it)
    h_ref, mix_ref, ext_p_ref, ext_s_ref, ext_c_ref, cc_ref, proj_s_ref = (
        next(it), next(it), next(it), next(it), next(it), next(it), next(it))
    if not has_state:
        shift_ref = next(it)

    s = tstride
    hp = ext_p_ref.shape[0] - tm
    hs = ext_s_ref.shape[0] - tm
    hc = ext_c_ref.shape[0] - tm
    i = pl.program_id(0)

    if has_state:
        ext_p_ref[0:hp, :] = _rows(st_pool_ref)
        ext_s_ref[0:hs, :] = _rows(st_sconv_ref)
        ext_c_ref[0:hc, :] = _rows(st_cconv_ref)
    else:
        @pl.when(i % tiles_per_seq == 0)
        def _():
            ext_p_ref[0:hp, :] = jnp.zeros((hp, D_POOL), F32)
            ext_s_ref[0:hs, :] = jnp.zeros((hs, D_SCONV), F32)
            ext_c_ref[0:hc, :] = jnp.zeros((hc, D_CCONV), F32)

    if round_ffn_weights:
        w_up_o_ref[...] = w_up_ref[...].astype(BF16)
        w_down_o_ref[...] = w_down_ref[...].astype(BF16)

    x = _rows(x_ref)
    h_ref[...] = (x * _rms_scale(x) * g1_ref[...]).astype(BF16)

    def in_proj(col0, c):
        return _dot(h_ref[...], w_in_ref[:, col0 + c * MXU_N:col0 + (c + 1) * MXU_N])

    for c in range(D_CCONV // MXU_N):
        ln = slice(c * MXU_N, (c + 1) * MXU_N)
        ext_c_ref[hc:hc + tm, ln] = in_proj(O4, c) * jax.nn.sigmoid(in_proj(O5, c))

    for c in range(3 * D_SCONV // MXU_N):
        proj_s_ref[:, c * MXU_N:(c + 1) * MXU_N] = in_proj(O1, c)
    for c in range(D_POOL // MXU_N):
        ext_p_ref[hp:hp + tm, c * MXU_N:(c + 1) * MXU_N] = in_proj(0, c)

    for lt in range(D_CCONV // LANES):
        ln = slice(lt * LANES, (lt + 1) * LANES)
        if s == 1:
            _shift_rows(ext_c_ref, shift_ref, ln)
            _conv_unit_stride(ext_c_ref, shift_ref, cconv_w_ref, cconv_b_ref, cc_ref, ln, hc, tm)
        else:
            _conv_aligned_stride(ext_c_ref, cconv_w_ref, cconv_b_ref, cc_ref, ln, hc, tm, s)
    cc = cc_ref[...]
    z = cc * _rms_scale(cc) * cconv_g_ref[...]
    mix_ref[:, O2:D_MODEL] = (z * jax.nn.sigmoid(z)).astype(BF16)

    gate_b = proj_s_ref[:, 0:D_SCONV]
    q = proj_s_ref[:, D_SCONV:2 * D_SCONV] * proj_s_ref[:, 2 * D_SCONV:3 * D_SCONV]
    ext_s_ref[hs:hs + tm, :] = q
    conv = sconv_w_ref[SCONV_K - 1:SCONV_K, :] * q
    for t in range(SCONV_K - 1):
        back = (SCONV_K - 1 - t) * s
        conv = conv + sconv_w_ref[t:t + 1, :] * ext_s_ref[hs - back:hs - back + tm, :]
    mix_ref[:, O1:O2] = (gate_b * conv).astype(BF16)

    if not has_state:
        pos = (i % tiles_per_seq) * tm + lax.broadcasted_iota(jnp.int32, (tm, 1), 0)
    for g, k in enumerate(POOL_WINDOWS):
        ln = slice(g * POOL_GROUP, (g + 1) * POOL_GROUP)
        u = ext_p_ref[hp:hp + tm, ln]
        win = u
        for m in range(1, k):
            win = win + ext_p_ref[hp - m * s:hp - m * s + tm, ln]
        if has_state:
            pooled = win / float(k)
        else:
            pooled = win / jnp.minimum(pos + 1, k).astype(F32)
        d = (pooled - u).astype(BF16)
        y = _dot(d, pool_w_ref[g]) * pool_scale_ref[:, ln]
        mix_ref[:, ln] = y.astype(BF16)

    _store_rows(o_ref, _rows(x_ref) + _dot(mix_ref[...], w_out_ref[...]))

    nb_p, nb_s, nb_c = POOL_BUF * s, (SCONV_K - 1) * s, (CCONV_K - 1) * s
    new_p = ext_p_ref[hp + tm - nb_p:hp + tm, :]
    new_s = ext_s_ref[hs + tm - nb_s:hs + tm, :]
    new_c = ext_c_ref[hc + tm - nb_c:hc + tm, :]
    if has_state:
        _store_rows(pool_o_ref, new_p)
        _store_rows(sconv_o_ref, new_s)
        _store_rows(cconv_o_ref, new_c)
    else:
        pool_o_ref[0] = new_p
        sconv_o_ref[0] = new_s
        cconv_o_ref[0] = new_c
        ext_p_ref[0:hp, :] = ext_p_ref[tm:tm + hp, :]
        ext_s_ref[0:hs, :] = ext_s_ref[tm:tm + hs, :]
        ext_c_ref[0:hc, :] = ext_c_ref[tm:tm + hc, :]


def _layer_spec(shape, layer, single_buffer=False):
    idx = lambda *_: (layer,) + (0,) * len(shape)
    if single_buffer:
        return pl.BlockSpec((None,) + shape, idx, pipeline_mode=pl.Buffered(1))
    return pl.BlockSpec((None,) + shape, idx)


def _mixer_call(x, states, w, layer, *, tm, tstride, tiles_per_seq, n_seq, ffn_weights=None):
    has_state = states is not None
    round_ffn_weights = ffn_weights is not None
    widths = [D_POOL, D_SCONV, D_CCONV]
    bufs = [POOL_BUF, SCONV_K - 1, CCONV_K - 1]
    if has_state:
        t_steps, n_b, _ = x.shape
        n_tiles = n_b // tstride
        halo = [nb * tstride for nb in bufs]
        x_spec = pl.BlockSpec((t_steps, tstride, D_MODEL), lambda i: (0, i, 0))
    else:
        n_tiles = x.shape[0] // tm
        halo = [_round_up(nb, SUBLANES) for nb in bufs]
        x_spec = pl.BlockSpec((tm, D_MODEL), lambda i: (i, 0))

    in_specs = [x_spec]
    args = [x]
    if has_state:
        for st, nb, wd in zip(states, bufs, widths):
            in_specs.append(pl.BlockSpec((None, nb, tstride, wd), lambda i: (layer, 0, i, 0),
                                         pipeline_mode=pl.Buffered(1)))
            args.append(st)
    in_specs += [
        _layer_spec((1, D_MODEL), layer),
        _layer_spec((D_MODEL, D_IN), layer, single_buffer=True),
        _layer_spec((len(POOL_WINDOWS), POOL_GROUP, POOL_GROUP), layer),
        _layer_spec((1, D_POOL), layer),
        _layer_spec((SCONV_K, D_SCONV), layer),
        _layer_spec((CCONV_K, D_CCONV), layer),
        _layer_spec((1, D_CCONV), layer),
        _layer_spec((1, D_CCONV), layer),
        _layer_spec((D_MODEL, D_MODEL), layer, single_buffer=True),
    ]
    args += [w["norm1_g"], w["w_in"], w["pool_w"], w["pool_scale"], w["sconv_w"],
             w["cconv_w"], w["cconv_b"], w["cconv_norm_g"], w["w_out"]]
    if round_ffn_weights:
        slabs = [(D_MODEL // n_tiles, 2 * D_FF), (D_FF // n_tiles, D_MODEL)]
        for wt, slab in zip(ffn_weights, slabs):
            in_specs.append(pl.BlockSpec((None,) + slab, lambda i: (layer, i, 0)))
            args.append(wt)

    out_shape = [jax.ShapeDtypeStruct(x.shape, F32)]
    out_specs = [x_spec]
    for nb, wd in zip(bufs, widths):
        if has_state:
            out_shape.append(jax.ShapeDtypeStruct((nb, n_b, wd), F32))
            out_specs.append(pl.BlockSpec((nb, tstride, wd), lambda i: (0, i, 0)))
        else:
            out_shape.append(jax.ShapeDtypeStruct((n_seq, nb, wd), F32))
            out_specs.append(pl.BlockSpec((1, nb, wd), lambda i: (i // tiles_per_seq, 0, 0)))

    if round_ffn_weights:
        for slab in slabs:
            out_shape.append(jax.ShapeDtypeStruct((slab[0] * n_tiles, slab[1]), BF16))
            out_specs.append(pl.BlockSpec(slab, lambda i: (i, 0)))

    scratch = [
        pltpu.VMEM((tm, D_MODEL), BF16),
        pltpu.VMEM((tm, D_MODEL), BF16),
        pltpu.VMEM((halo[0] + tm, D_POOL), F32),
        pltpu.VMEM((halo[1] + tm, D_SCONV), F32),
        pltpu.VMEM((halo[2] + tm, D_CCONV), F32),
        pltpu.VMEM((tm, D_CCONV), F32),
        pltpu.VMEM((tm, 3 * D_SCONV), F32),
    ]
    if not has_state:
        scratch.append(pltpu.VMEM((SUBLANES - 1, halo[2] + tm, D_CCONV), F32))
    body = functools.partial(_mixer_body, tm=tm, tstride=tstride, tiles_per_seq=tiles_per_seq,
                             has_state=has_state, round_ffn_weights=round_ffn_weights)
    return pl.pallas_call(
        body,
        grid=(n_tiles,),
        in_specs=in_specs,
        out_specs=out_specs,
        out_shape=out_shape,
        scratch_shapes=scratch,
        compiler_params=pltpu.CompilerParams(
            dimension_semantics=("arbitrary",), vmem_limit_bytes=VMEM_LIMIT_BYTES),
        name="mixer_state" if has_state else "mixer_prompt",
    )(*args)


def _row_block(ref, r0, rows):
    if len(ref.shape) == 2:
        return ref.at[r0:r0 + rows]
    n_b = ref.shape[1]
    return ref.at[r0 // n_b:(r0 + rows) // n_b]


def _ffn_body(*refs, tm, tstride, tiles_per_seq, has_state, final_norm):
    it = iter(refs)
    x_ref = next(it)
    if has_state:
        st_a_ref, st_g_ref = next(it), next(it)
    g2_ref, wa_ref, wg_ref, cwa_ref, cwg_ref, wd_ref = (
        next(it), next(it), next(it), next(it), next(it), next(it))
    if final_norm:
        gf_ref = next(it)
    o_ref, new_a_ref, new_g_ref = next(it), next(it), next(it)
    h_ref, ext_ref = next(it), next(it)
    if not has_state:
        halo_ref = next(it)

    s = tstride
    ck = wa_ref.shape[1]
    hl = ext_ref.shape[0] - tm
    rows = tm // FFN_ROW_SPLITS
    i = pl.program_id(0)
    j = pl.program_id(1)
    nj = pl.num_programs(1)

    @pl.when(j == 0)
    def _():
        x = _rows(x_ref)
        h_ref[...] = (x * _rms_scale(x) * g2_ref[...]).astype(BF16)
        _store_rows(o_ref, x)
        if not has_state:
            @pl.when(i % tiles_per_seq == 0)
            def _():
                halo_ref[...] = jnp.zeros(halo_ref.shape, F32)

    if has_state:
        ext_ref[0:hl, 0:ck] = _rows(st_a_ref)
        ext_ref[0:hl, ck:2 * ck] = _rows(st_g_ref)
    else:
        ext_ref[0:hl, :] = halo_ref[j]

    w_a, w_g, w_d = wa_ref[...], wg_ref[...], wd_ref[...]
    for r in range(FFN_ROW_SPLITS):
        r0 = r * rows
        h = h_ref[r0:r0 + rows, :]
        ext_ref[hl + r0:hl + r0 + rows, 0:ck] = _dot(h, w_a)
        ext_ref[hl + r0:hl + r0 + rows, ck:2 * ck] = _dot(h, w_g)

    def conv(cw_ref, r0, lo):
        out = None
        for t in range(FFN_CONV_K):
            start = hl + r0 - (FFN_CONV_K - 1 - t) * s
            term = cw_ref[t:t + 1, :] * ext_ref[start:start + rows, lo:lo + ck]
            out = term if out is None else out + term
        return out

    for r in range(FFN_ROW_SPLITS):
        r0 = r * rows
        c_a = conv(cwa_ref, r0, 0)
        c_g = conv(cwg_ref, r0, ck)
        act = (c_a * jax.nn.sigmoid(c_a) * c_g).astype(BF16)
        o_blk = _row_block(o_ref, r0, rows)
        _store_rows(o_blk, _rows(o_blk) + _dot(act, w_d))

    nb = (FFN_CONV_K - 1) * s
    new_a = ext_ref[hl + tm - nb:hl + tm, 0:ck]
    new_g = ext_ref[hl + tm - nb:hl + tm, ck:2 * ck]
    if has_state:
        _store_rows(new_a_ref, new_a)
        _store_rows(new_g_ref, new_g)
    else:
        seq = i // tiles_per_seq
        new_a_ref[seq, j] = new_a
        new_g_ref[seq, j] = new_g
        halo_ref[j] = ext_ref[tm:tm + hl, :]

    if final_norm:
        @pl.when(j == nj - 1)
        def _():
            y = _rows(o_ref)
            _store_rows(o_ref, y * _rms_scale(y) * gf_ref[...])


def _ffn_call(x, state, w, w_up_b, w_down_b, layer, final_g, *, tm, tstride, tiles_per_seq, n_seq):
    has_state = state is not None
    final_norm = final_g is not None
    ck = FFN_CK
    nj = D_FF // ck
    nb = FFN_CONV_K - 1
    if has_state:
        t_steps, n_b, _ = x.shape
        n_tiles = n_b // tstride
        hl = nb * tstride
        x_block, x_index = (t_steps, tstride, D_MODEL), lambda i, j: (0, i, 0)
    else:
        n_tiles = x.shape[0] // tm
        hl = _round_up(nb, SUBLANES)
        x_block, x_index = (tm, D_MODEL), lambda i, j: (i, 0)

    in_specs = [pl.BlockSpec(x_block, x_index, pipeline_mode=pl.Buffered(1))]
    args = [x]
    if has_state:
        in_specs += [pl.BlockSpec((None, nb, tstride, ck), lambda i, j: (layer, 0, i, j)),
                     pl.BlockSpec((None, nb, tstride, ck), lambda i, j: (layer, 0, i, nj + j))]
        args += [state, state]
    in_specs += [
        _layer_spec((1, D_MODEL), layer),
        pl.BlockSpec((D_MODEL, ck), lambda i, j: (0, j)),
        pl.BlockSpec((D_MODEL, ck), lambda i, j: (0, nj + j)),
        pl.BlockSpec((None, FFN_CONV_K, ck), lambda i, j: (layer, 0, j)),
        pl.BlockSpec((None, FFN_CONV_K, ck), lambda i, j: (layer, 0, nj + j)),
        pl.BlockSpec((ck, D_MODEL), lambda i, j: (j, 0)),
    ]
    args += [w["norm2_g"], w_up_b, w_up_b, w["ffn_conv_w"], w["ffn_conv_w"], w_down_b]
    if final_norm:
        in_specs.append(pl.BlockSpec((1, D_MODEL), lambda i, j: (0, 0)))
        args.append(final_g)

    out_shape = [jax.ShapeDtypeStruct(x.shape, F32)]
    out_specs = [pl.BlockSpec(x_block, x_index)]
    for _ in range(2):
        if has_state:
            out_shape.append(jax.ShapeDtypeStruct((nb, n_b, D_FF), F32))
            out_specs.append(pl.BlockSpec((nb, tstride, ck), lambda i, j: (0, i, j)))
        else:
            out_shape.append(jax.ShapeDtypeStruct((n_seq, nj, nb, ck), F32))
            out_specs.append(pl.BlockSpec((n_seq, nj, nb, ck), lambda i, j: (0, 0, 0, 0)))

    scratch = [pltpu.VMEM((tm, D_MODEL), BF16), pltpu.VMEM((hl + tm, 2 * ck), F32)]
    if not has_state:
        scratch.append(pltpu.VMEM((nj, hl, 2 * ck), F32))
    body = functools.partial(_ffn_body, tm=tm, tstride=tstride, tiles_per_seq=tiles_per_seq,
                             has_state=has_state, final_norm=final_norm)
    return pl.pallas_call(
        body,
        grid=(n_tiles, nj),
        in_specs=in_specs,
        out_specs=out_specs,
        out_shape=out_shape,
        scratch_shapes=scratch,
        compiler_params=pltpu.CompilerParams(
            dimension_semantics=("arbitrary", "arbitrary"), vmem_limit_bytes=VMEM_LIMIT_BYTES),
        name="ffn_state" if has_state else "ffn_prompt",
    )(*args)


def _time_major(a):
    return jnp.swapaxes(a, -3, -2)


def kernel(x_prompt, x_sample, state_pool, state_sconv, state_cconv, state_ffn, norm1_g, w_in,
           pool_w, pool_scale, sconv_w, cconv_w, cconv_b, cconv_norm_g, w_out, norm2_g, w_up,
           ffn_conv_w, w_down, final_norm_g):
    batch, seq, _ = x_prompt.shape
    dec_batch, dec_seq, _ = x_sample.shape
    depth = w_in.shape[0]

    w = {
        "norm1_g": norm1_g.reshape(depth, 1, D_MODEL),
        "w_in": w_in.astype(BF16),
        "pool_w": pool_w.astype(BF16),
        "pool_scale": pool_scale.reshape(depth, 1, D_POOL),
        "sconv_w": sconv_w,
        "cconv_w": cconv_w,
        "cconv_b": cconv_b.reshape(depth, 1, D_CCONV),
        "cconv_norm_g": cconv_norm_g.reshape(depth, 1, D_CCONV),
        "w_out": w_out.astype(BF16),
        "norm2_g": norm2_g.reshape(depth, 1, D_MODEL),
        "ffn_conv_w": ffn_conv_w,
    }
    final_g = final_norm_g.reshape(1, D_MODEL)

    xp = x_prompt.reshape(batch * seq, D_MODEL)
    xs = _time_major(x_sample)
    st_mix = [_time_major(state_pool), _time_major(state_sconv), _time_major(state_cconv)]
    st_ffn = _time_major(state_ffn)

    outs_p = [[], [], [], []]
    outs_s = [[], [], [], []]
    for l in range(depth):
        last = final_g if l == depth - 1 else None

        xp, pool_p, sconv_p, cconv_p, w_up_b, w_down_b = _mixer_call(
            xp, None, w, l, tm=MIX_TM_PROMPT, tstride=1,
            tiles_per_seq=seq // MIX_TM_PROMPT, n_seq=batch, ffn_weights=(w_up, w_down))
        xp, ffa_p, ffg_p = _ffn_call(
            xp, None, w, w_up_b, w_down_b, l, last, tm=FFN_TM_PROMPT, tstride=1,
            tiles_per_seq=seq // FFN_TM_PROMPT, n_seq=batch)
        unchunk = lambda a: a.transpose(0, 2, 1, 3).reshape(batch, FFN_CONV_K - 1, D_FF)
        outs_p[0].append(pool_p)
        outs_p[1].append(sconv_p)
        outs_p[2].append(cconv_p)
        outs_p[3].append(jnp.concatenate([unchunk(ffa_p), unchunk(ffg_p)], axis=-1))

        xs, pool_s, sconv_s, cconv_s = _mixer_call(
            xs, st_mix, w, l, tm=dec_seq * MIX_SAMPLE_SEQS, tstride=MIX_SAMPLE_SEQS,
            tiles_per_seq=1, n_seq=dec_batch)
        xs, ffa_s, ffg_s = _ffn_call(
            xs, st_ffn, w, w_up_b, w_down_b, l, last, tm=dec_seq * dec_batch, tstride=dec_batch,
            tiles_per_seq=1, n_seq=dec_batch)
        outs_s[0].append(pool_s)
        outs_s[1].append(sconv_s)
        outs_s[2].append(cconv_s)
        outs_s[3].append(jnp.concatenate([ffa_s, ffg_s], axis=-1))

    y_prompt = xp.reshape(batch, seq, D_MODEL)
    y_sample = _time_major(xs)
    stack_tm = lambda parts: _time_major(jnp.stack(parts))
    return (y_prompt, y_sample,
            jnp.stack(outs_p[0]), jnp.stack(outs_p[1]), jnp.stack(outs_p[2]), jnp.stack(outs_p[3]),
            stack_tm(outs_s[0]), stack_tm(outs_s[1]), stack_tm(outs_s[2]), stack_tm(outs_s[3]))
```

```python
import functools

import jax
import jax.numpy as jnp
from jax import lax
from jax.experimental import pallas as pl
from jax.experimental.pallas import tpu as pltpu

EPS = 1e-6
D_MODEL = 2048
D_POOL = 512
D_SCONV = 768
D_CCONV = 768
POOL_WINDOWS = (2, 4, 8, 16)
POOL_GROUP = 128
POOL_BUF = 15
SCONV_K = 3
CCONV_K = 31
FFN_CONV_K = 3
D_FF = 5632
D_IN = D_POOL + 3 * D_SCONV + 2 * D_CCONV
O1 = D_POOL
O2 = O1 + D_SCONV
O3 = O2 + D_SCONV
O4 = O3 + D_SCONV
O5 = O4 + D_CCONV

SUBLANES = 8
LANES = 128
MXU_N = 256
VMEM_LIMIT_BYTES = 56 * 1024 * 1024

MIX_TM_PROMPT = 256
MIX_SAMPLE_SEQS = 32
FFN_TM_PROMPT = 1024
FFN_CK = 512
FFN_ROW_SPLITS = 4
CONV_ROW_BLOCK = 32

BF16 = jnp.bfloat16
F32 = jnp.float32


def _round_up(n, m):
    return (n + m - 1) // m * m


def _dot(a, b):
    return jnp.dot(a, b, preferred_element_type=F32)


def _rms_scale(x):
    return lax.rsqrt(jnp.mean(x * x, axis=-1, keepdims=True) + EPS)


def _rows(ref, width):
    parts = [ref[:, c:c + width] for c in range(0, ref.shape[1], width)]
    return parts[0] if len(parts) == 1 else jnp.concatenate(parts, axis=0)


def _store_rows(ref, v):
    n_b, width = ref.shape[0], v.shape[1]
    for t in range(ref.shape[1] // width):
        ref[:, t * width:(t + 1) * width] = v[t * n_b:(t + 1) * n_b, :]


def _conv_aligned_stride(ext_ref, w_ref, b_ref, out_ref, ln, halo, tm, tstride):
    k = w_ref.shape[0]
    for rb in range(tm // CONV_ROW_BLOCK):
        r0 = rb * CONV_ROW_BLOCK
        acc = jnp.broadcast_to(b_ref[:, ln], (CONV_ROW_BLOCK, LANES))
        for t in range(k):
            start = halo - (k - 1 - t) * tstride + r0
            acc = acc + w_ref[t:t + 1, ln] * ext_ref[start:start + CONV_ROW_BLOCK, ln]
        out_ref[r0:r0 + CONV_ROW_BLOCK, ln] = acc


def _shift_rows(ext_ref, shift_ref, ln):
    row = lax.broadcasted_iota(jnp.int32, (SUBLANES, LANES), 0)
    prev = None
    for q in range(ext_ref.shape[0] // SUBLANES):
        t = ext_ref[q * SUBLANES:(q + 1) * SUBLANES, ln]
        rolled = [pltpu.roll(t, r, axis=0) for r in range(1, SUBLANES)]
        if prev is not None:
            for r in range(1, SUBLANES):
                shift_ref[r - 1, q * SUBLANES:(q + 1) * SUBLANES, ln] = jnp.where(
                    row >= r, rolled[r - 1], prev[r - 1])
        prev = rolled


def _conv_unit_stride(ext_ref, shift_ref, w_ref, b_ref, out_ref, ln, halo, tm):
    k = w_ref.shape[0]
    for rb in range(tm // CONV_ROW_BLOCK):
        r0 = rb * CONV_ROW_BLOCK
        acc = jnp.broadcast_to(b_ref[:, ln], (CONV_ROW_BLOCK, LANES))
        for d in range(k):
            a, r = divmod(d, SUBLANES)
            start = halo + r0 - a * SUBLANES
            src = ext_ref if r == 0 else shift_ref.at[r - 1]
            acc = acc + w_ref[k - 1 - d:k - d, ln] * src[start:start + CONV_ROW_BLOCK, ln]
        out_ref[r0:r0 + CONV_ROW_BLOCK, ln] = acc


def _mixer_body(*refs, tm, tstride, tiles_per_seq, has_state, round_ffn_weights):
    it = iter(refs)
    x_ref = next(it)
    if has_state:
        st_pool_ref, st_sconv_ref, st_cconv_ref = next(it), next(it), next(it)
    g1_ref, w_in_ref, pool_w_ref, pool_scale_ref = next(it), next(it), next(it), next(it)
    sconv_w_ref, cconv_w_ref, cconv_b_ref, cconv_g_ref, w_out_ref = (
        next(it), next(it), next(it), next(it), next(it))
    if round_ffn_weights:
        w_up_ref, w_down_ref = next(it), next(it)
    o_ref, pool_o_ref, sconv_o_ref, cconv_o_ref = next(it), next(it), next(it), next(it)
    if round_ffn_weights:
        w_up_o_ref, w_down_o_ref = next(it), next(it)
    h_ref, mix_ref, ext_p_ref, ext_s_ref, ext_c_ref, cc_ref, proj_s_ref = (
        next(it), next(it), next(it), next(it), next(it), next(it), next(it))
    if not has_state:
        shift_ref = next(it)

    s = tstride
    hp = ext_p_ref.shape[0] - tm
    hs = ext_s_ref.shape[0] - tm
    hc = ext_c_ref.shape[0] - tm
    i = pl.program_id(0)

    if has_state:
        ext_p_ref[0:hp, :] = _rows(st_pool_ref, D_POOL)
        ext_s_ref[0:hs, :] = _rows(st_sconv_ref, D_SCONV)
        ext_c_ref[0:hc, :] = _rows(st_cconv_ref, D_CCONV)
    else:
        @pl.when(i % tiles_per_seq == 0)
        def _():
            ext_p_ref[0:hp, :] = jnp.zeros((hp, D_POOL), F32)
            ext_s_ref[0:hs, :] = jnp.zeros((hs, D_SCONV), F32)
            ext_c_ref[0:hc, :] = jnp.zeros((hc, D_CCONV), F32)

    if round_ffn_weights:
        w_up_o_ref[...] = w_up_ref[...].astype(BF16)
        w_down_o_ref[...] = w_down_ref[...].astype(BF16)

    x = _rows(x_ref, D_MODEL)
    h_ref[...] = (x * _rms_scale(x) * g1_ref[...]).astype(BF16)

    def in_proj(col0, c):
        return _dot(h_ref[...], w_in_ref[:, col0 + c * MXU_N:col0 + (c + 1) * MXU_N])

    for c in range(D_CCONV // MXU_N):
        ln = slice(c * MXU_N, (c + 1) * MXU_N)
        ext_c_ref[hc:hc + tm, ln] = in_proj(O4, c) * jax.nn.sigmoid(in_proj(O5, c))

    for c in range(3 * D_SCONV // MXU_N):
        proj_s_ref[:, c * MXU_N:(c + 1) * MXU_N] = in_proj(O1, c)
    for c in range(D_POOL // MXU_N):
        ext_p_ref[hp:hp + tm, c * MXU_N:(c + 1) * MXU_N] = in_proj(0, c)

    for lt in range(D_CCONV // LANES):
        ln = slice(lt * LANES, (lt + 1) * LANES)
        if s == 1:
            _shift_rows(ext_c_ref, shift_ref, ln)
            _conv_unit_stride(ext_c_ref, shift_ref, cconv_w_ref, cconv_b_ref, cc_ref, ln, hc, tm)
        else:
            _conv_aligned_stride(ext_c_ref, cconv_w_ref, cconv_b_ref, cc_ref, ln, hc, tm, s)
    cc = cc_ref[...]
    z = cc * _rms_scale(cc) * cconv_g_ref[...]
    mix_ref[:, O2:D_MODEL] = (z * jax.nn.sigmoid(z)).astype(BF16)

    gate_b = proj_s_ref[:, 0:D_SCONV]
    q = proj_s_ref[:, D_SCONV:2 * D_SCONV] * proj_s_ref[:, 2 * D_SCONV:3 * D_SCONV]
    ext_s_ref[hs:hs + tm, :] = q
    conv = sconv_w_ref[SCONV_K - 1:SCONV_K, :] * q
    for t in range(SCONV_K - 1):
        back = (SCONV_K - 1 - t) * s
        conv = conv + sconv_w_ref[t:t + 1, :] * ext_s_ref[hs - back:hs - back + tm, :]
    mix_ref[:, O1:O2] = (gate_b * conv).astype(BF16)

    if not has_state:
        pos = (i % tiles_per_seq) * tm + lax.broadcasted_iota(jnp.int32, (tm, 1), 0)
    for g, k in enumerate(POOL_WINDOWS):
        ln = slice(g * POOL_GROUP, (g + 1) * POOL_GROUP)
        u = ext_p_ref[hp:hp + tm, ln]
        win = u
        for m in range(1, k):
            win = win + ext_p_ref[hp - m * s:hp - m * s + tm, ln]
        if has_state:
            pooled = win / float(k)
        else:
            pooled = win / jnp.minimum(pos + 1, k).astype(F32)
        d = (pooled - u).astype(BF16)
        y = _dot(d, pool_w_ref[g]) * pool_scale_ref[:, ln]
        mix_ref[:, ln] = y.astype(BF16)

    _store_rows(o_ref, _rows(x_ref, D_MODEL) + _dot(mix_ref[...], w_out_ref[...]))

    nb_p, nb_s, nb_c = POOL_BUF * s, (SCONV_K - 1) * s, (CCONV_K - 1) * s
    new_p = ext_p_ref[hp + tm - nb_p:hp + tm, :]
    new_s = ext_s_ref[hs + tm - nb_s:hs + tm, :]
    new_c = ext_c_ref[hc + tm - nb_c:hc + tm, :]
    if has_state:
        _store_rows(pool_o_ref, new_p)
        _store_rows(sconv_o_ref, new_s)
        _store_rows(cconv_o_ref, new_c)
    else:
        pool_o_ref[0] = new_p
        sconv_o_ref[0] = new_s
        cconv_o_ref[0] = new_c
        ext_p_ref[0:hp, :] = ext_p_ref[tm:tm + hp, :]
        ext_s_ref[0:hs, :] = ext_s_ref[tm:tm + hs, :]
        ext_c_ref[0:hc, :] = ext_c_ref[tm:tm + hc, :]


def _layer_spec(shape, layer, single_buffer=False):
    idx = lambda *_: (layer,) + (0,) * len(shape)
    if single_buffer:
        return pl.BlockSpec((None,) + shape, idx, pipeline_mode=pl.Buffered(1))
    return pl.BlockSpec((None,) + shape, idx)


def _mixer_call(x, states, w, layer, *, tm, tstride, tiles_per_seq, n_seq, ffn_weights=None):
    has_state = states is not None
    round_ffn_weights = ffn_weights is not None
    widths = [D_POOL, D_SCONV, D_CCONV]
    bufs = [POOL_BUF, SCONV_K - 1, CCONV_K - 1]
    if has_state:
        n_b = x.shape[0]
        n_tiles = n_b // tstride
        halo = [nb * tstride for nb in bufs]
        x_spec = pl.BlockSpec((tstride, x.shape[1]), lambda i: (i, 0))
    else:
        n_tiles = x.shape[0] // tm
        halo = [_round_up(nb, SUBLANES) for nb in bufs]
        x_spec = pl.BlockSpec((tm, D_MODEL), lambda i: (i, 0))

    in_specs = [x_spec]
    args = [x]
    if has_state:
        for st, nb, wd in zip(states, bufs, widths):
            in_specs.append(pl.BlockSpec((None, tstride, nb * wd), lambda i: (layer, i, 0),
                                         pipeline_mode=pl.Buffered(1)))
            args.append(st)
    in_specs += [
        _layer_spec((1, D_MODEL), layer),
        _layer_spec((D_MODEL, D_IN), layer, single_buffer=True),
        _layer_spec((len(POOL_WINDOWS), POOL_GROUP, POOL_GROUP), layer),
        _layer_spec((1, D_POOL), layer),
        _layer_spec((SCONV_K, D_SCONV), layer),
        _layer_spec((CCONV_K, D_CCONV), layer),
        _layer_spec((1, D_CCONV), layer),
        _layer_spec((1, D_CCONV), layer),
        _layer_spec((D_MODEL, D_MODEL), layer, single_buffer=True),
    ]
    args += [w["norm1_g"], w["w_in"], w["pool_w"], w["pool_scale"], w["sconv_w"],
             w["cconv_w"], w["cconv_b"], w["cconv_norm_g"], w["w_out"]]
    if round_ffn_weights:
        slabs = [(D_MODEL // n_tiles, 2 * D_FF), (D_FF // n_tiles, D_MODEL)]
        for wt, slab in zip(ffn_weights, slabs):
            in_specs.append(pl.BlockSpec((None,) + slab, lambda i: (layer, i, 0)))
            args.append(wt)

    out_shape = [jax.ShapeDtypeStruct(x.shape, F32)]
    out_specs = [x_spec]
    for nb, wd in zip(bufs, widths):
        if has_state:
            out_shape.append(jax.ShapeDtypeStruct((n_b, nb * wd), F32))
            out_specs.append(pl.BlockSpec((tstride, nb * wd), lambda i: (i, 0)))
        else:
            out_shape.append(jax.ShapeDtypeStruct((n_seq, nb, wd), F32))
            out_specs.append(pl.BlockSpec((1, nb, wd), lambda i: (i // tiles_per_seq, 0, 0)))

    if round_ffn_weights:
        for slab in slabs:
            out_shape.append(jax.ShapeDtypeStruct((slab[0] * n_tiles, slab[1]), BF16))
            out_specs.append(pl.BlockSpec(slab, lambda i: (i, 0)))

    scratch = [
        pltpu.VMEM((tm, D_MODEL), BF16),
        pltpu.VMEM((tm, D_MODEL), BF16),
        pltpu.VMEM((halo[0] + tm, D_POOL), F32),
        pltpu.VMEM((halo[1] + tm, D_SCONV), F32),
        pltpu.VMEM((halo[2] + tm, D_CCONV), F32),
        pltpu.VMEM((tm, D_CCONV), F32),
        pltpu.VMEM((tm, 3 * D_SCONV), F32),
    ]
    if not has_state:
        scratch.append(pltpu.VMEM((SUBLANES - 1, halo[2] + tm, D_CCONV), F32))
    body = functools.partial(_mixer_body, tm=tm, tstride=tstride, tiles_per_seq=tiles_per_seq,
                             has_state=has_state, round_ffn_weights=round_ffn_weights)
    return pl.pallas_call(
        body,
        grid=(n_tiles,),
        in_specs=in_specs,
        out_specs=out_specs,
        out_shape=out_shape,
        scratch_shapes=scratch,
        compiler_params=pltpu.CompilerParams(
            dimension_semantics=("arbitrary",), vmem_limit_bytes=VMEM_LIMIT_BYTES),
        name="mixer_state" if has_state else "mixer_prompt",
    )(*args)


def _row_block(ref, r0, rows):
    if ref.shape[1] == D_MODEL:
        return ref.at[r0:r0 + rows]
    n_b = ref.shape[0]
    return ref.at[:, r0 // n_b * D_MODEL:(r0 + rows) // n_b * D_MODEL]


def _ffn_body(*refs, tm, tstride, tiles_per_seq, has_state, final_norm):
    it = iter(refs)
    x_ref = next(it)
    nb = FFN_CONV_K - 1
    if has_state:
        st_a_refs = [next(it) for _ in range(nb)]
        st_g_refs = [next(it) for _ in range(nb)]
    g2_ref, wa_ref, wg_ref, cwa_ref, cwg_ref, wd_ref = (
        next(it), next(it), next(it), next(it), next(it), next(it))
    if final_norm:
        gf_ref = next(it)
    o_ref = next(it)
    if has_state:
        new_a_refs = [next(it) for _ in range(nb)]
        new_g_refs = [next(it) for _ in range(nb)]
    else:
        new_a_ref, new_g_ref = next(it), next(it)
    h_ref, ext_ref = next(it), next(it)
    if not has_state:
        halo_ref = next(it)

    s = tstride
    ck = wa_ref.shape[1]
    hl = ext_ref.shape[0] - tm
    rows = tm // FFN_ROW_SPLITS
    i = pl.program_id(0)
    j = pl.program_id(1)
    nj = pl.num_programs(1)

    @pl.when(j == 0)
    def _():
        x = _rows(x_ref, D_MODEL)
        h_ref[...] = (x * _rms_scale(x) * g2_ref[...]).astype(BF16)
        _store_rows(o_ref, x)
        if not has_state:
            @pl.when(i % tiles_per_seq == 0)
            def _():
                halo_ref[...] = jnp.zeros(halo_ref.shape, F32)

    if has_state:
        for t in range(nb):
            ext_ref[t * s:(t + 1) * s, 0:ck] = st_a_refs[t][...]
            ext_ref[t * s:(t + 1) * s, ck:2 * ck] = st_g_refs[t][...]
    else:
        ext_ref[0:hl, :] = halo_ref[j]

    w_a, w_g, w_d = wa_ref[...], wg_ref[...], wd_ref[...]
    for r in range(FFN_ROW_SPLITS):
        r0 = r * rows
        h = h_ref[r0:r0 + rows, :]
        ext_ref[hl + r0:hl + r0 + rows, 0:ck] = _dot(h, w_a)
        ext_ref[hl + r0:hl + r0 + rows, ck:2 * ck] = _dot(h, w_g)

    def conv(cw_ref, r0, lo):
        out = None
        for t in range(FFN_CONV_K):
            start = hl + r0 - (FFN_CONV_K - 1 - t) * s
            term = cw_ref[t:t + 1, :] * ext_ref[start:start + rows, lo:lo + ck]
            out = term if out is None else out + term
        return out

    for r in range(FFN_ROW_SPLITS):
        r0 = r * rows
        c_a = conv(cwa_ref, r0, 0)
        c_g = conv(cwg_ref, r0, ck)
        act = (c_a * jax.nn.sigmoid(c_a) * c_g).astype(BF16)
        o_blk = _row_block(o_ref, r0, rows)
        _store_rows(o_blk, _rows(o_blk, D_MODEL) + _dot(act, w_d))

    first = hl + tm - nb * s
    if has_state:
        for t in range(nb):
            new_a_refs[t][...] = ext_ref[first + t * s:first + (t + 1) * s, 0:ck]
            new_g_refs[t][...] = ext_ref[first + t * s:first + (t + 1) * s, ck:2 * ck]
    else:
        seq = i // tiles_per_seq
        new_a_ref[seq, j] = ext_ref[first:hl + tm, 0:ck]
        new_g_ref[seq, j] = ext_ref[first:hl + tm, ck:2 * ck]
        halo_ref[j] = ext_ref[tm:tm + hl, :]

    if final_norm:
        @pl.when(j == nj - 1)
        def _():
            y = _rows(o_ref, D_MODEL)
            _store_rows(o_ref, y * _rms_scale(y) * gf_ref[...])


def _ffn_call(x, state, w, w_up_b, w_down_b, layer, final_g, *, tm, tstride, tiles_per_seq, n_seq):
    has_state = state is not None
    final_norm = final_g is not None
    ck = FFN_CK
    nj = D_FF // ck
    nb = FFN_CONV_K - 1
    if has_state:
        n_b = x.shape[0]
        n_tiles = n_b // tstride
        hl = nb * tstride
        x_block, x_index = (tstride, x.shape[1]), lambda i, j: (i, 0)
    else:
        n_tiles = x.shape[0] // tm
        hl = _round_up(nb, SUBLANES)
        x_block, x_index = (tm, D_MODEL), lambda i, j: (i, 0)

    in_specs = [pl.BlockSpec(x_block, x_index, pipeline_mode=pl.Buffered(1))]
    args = [x]
    if has_state:
        for half in range(2):
            for t in range(nb):
                first = (2 * t + half) * nj
                in_specs.append(pl.BlockSpec((None, tstride, ck),
                                             lambda i, j, first=first: (layer, i, first + j)))
                args.append(state)
    in_specs += [
        _layer_spec((1, D_MODEL), layer),
        pl.BlockSpec((D_MODEL, ck), lambda i, j: (0, j)),
        pl.BlockSpec((D_MODEL, ck), lambda i, j: (0, nj + j)),
        pl.BlockSpec((None, FFN_CONV_K, ck), lambda i, j: (layer, 0, j)),
        pl.BlockSpec((None, FFN_CONV_K, ck), lambda i, j: (layer, 0, nj + j)),
        pl.BlockSpec((ck, D_MODEL), lambda i, j: (j, 0)),
    ]
    args += [w["norm2_g"], w_up_b, w_up_b, w["ffn_conv_w"], w["ffn_conv_w"], w_down_b]
    if final_norm:
        in_specs.append(pl.BlockSpec((1, D_MODEL), lambda i, j: (0, 0)))
        args.append(final_g)

    out_shape = [jax.ShapeDtypeStruct(x.shape, F32)]
    out_specs = [pl.BlockSpec(x_block, x_index)]
    for _ in range(2):
        if has_state:
            for _ in range(nb):
                out_shape.append(jax.ShapeDtypeStruct((n_b, D_FF), F32))
                out_specs.append(pl.BlockSpec((tstride, ck), lambda i, j: (i, j)))
        else:
            out_shape.append(jax.ShapeDtypeStruct((n_seq, nj, nb, ck), F32))
            out_specs.append(pl.BlockSpec((n_seq, nj, nb, ck), lambda i, j: (0, 0, 0, 0)))

    scratch = [pltpu.VMEM((tm, D_MODEL), BF16), pltpu.VMEM((hl + tm, 2 * ck), F32)]
    if not has_state:
        scratch.append(pltpu.VMEM((nj, hl, 2 * ck), F32))
    body = functools.partial(_ffn_body, tm=tm, tstride=tstride, tiles_per_seq=tiles_per_seq,
                             has_state=has_state, final_norm=final_norm)
    return pl.pallas_call(
        body,
        grid=(n_tiles, nj),
        in_specs=in_specs,
        out_specs=out_specs,
        out_shape=out_shape,
        scratch_shapes=scratch,
        compiler_params=pltpu.CompilerParams(
            dimension_semantics=("arbitrary", "arbitrary"), vmem_limit_bytes=VMEM_LIMIT_BYTES),
        name="ffn_state" if has_state else "ffn_prompt",
    )(*args)


def kernel(x_prompt, x_sample, state_pool, state_sconv, state_cconv, state_ffn, norm1_g, w_in,
           pool_w, pool_scale, sconv_w, cconv_w, cconv_b, cconv_norm_g, w_out, norm2_g, w_up,
           ffn_conv_w, w_down, final_norm_g):
    batch, seq, _ = x_prompt.shape
    dec_batch, dec_seq, _ = x_sample.shape
    depth = w_in.shape[0]

    w = {
        "norm1_g": norm1_g.reshape(depth, 1, D_MODEL),
        "w_in": w_in.astype(BF16),
        "pool_w": pool_w.astype(BF16),
        "pool_scale": pool_scale.reshape(depth, 1, D_POOL),
        "sconv_w": sconv_w,
        "cconv_w": cconv_w,
        "cconv_b": cconv_b.reshape(depth, 1, D_CCONV),
        "cconv_norm_g": cconv_norm_g.reshape(depth, 1, D_CCONV),
        "w_out": w_out.astype(BF16),
        "norm2_g": norm2_g.reshape(depth, 1, D_MODEL),
        "ffn_conv_w": ffn_conv_w,
    }
    final_g = final_norm_g.reshape(1, D_MODEL)

    xp = x_prompt.reshape(batch * seq, D_MODEL)
    xs = x_sample.reshape(dec_batch, dec_seq * D_MODEL)
    st_mix = [state_pool.reshape(depth, dec_batch, POOL_BUF * D_POOL),
              state_sconv.reshape(depth, dec_batch, (SCONV_K - 1) * D_SCONV),
              state_cconv.reshape(depth, dec_batch, (CCONV_K - 1) * D_CCONV)]
    st_ffn = state_ffn.reshape(depth, dec_batch, (FFN_CONV_K - 1) * 2 * D_FF)

    outs_p = [[], [], [], []]
    outs_s = [[], [], [], []]
    for l in range(depth):
        last = final_g if l == depth - 1 else None

        xp, pool_p, sconv_p, cconv_p, w_up_b, w_down_b = _mixer_call(
            xp, None, w, l, tm=MIX_TM_PROMPT, tstride=1,
            tiles_per_seq=seq // MIX_TM_PROMPT, n_seq=batch, ffn_weights=(w_up, w_down))
        xp, ffa_p, ffg_p = _ffn_call(
            xp, None, w, w_up_b, w_down_b, l, last, tm=FFN_TM_PROMPT, tstride=1,
            tiles_per_seq=seq // FFN_TM_PROMPT, n_seq=batch)
        unchunk = lambda a: a.transpose(0, 2, 1, 3).reshape(batch, FFN_CONV_K - 1, D_FF)
        outs_p[0].append(pool_p)
        outs_p[1].append(sconv_p)
        outs_p[2].append(cconv_p)
        outs_p[3].append(jnp.concatenate([unchunk(ffa_p), unchunk(ffg_p)], axis=-1))

        xs, pool_s, sconv_s, cconv_s = _mixer_call(
            xs, st_mix, w, l, tm=dec_seq * MIX_SAMPLE_SEQS, tstride=MIX_SAMPLE_SEQS,
            tiles_per_seq=1, n_seq=dec_batch)
        xs, ffa0_s, ffa1_s, ffg0_s, ffg1_s = _ffn_call(
            xs, st_ffn, w, w_up_b, w_down_b, l, last, tm=dec_seq * dec_batch, tstride=dec_batch,
            tiles_per_seq=1, n_seq=dec_batch)
        outs_s[0].append(pool_s.reshape(dec_batch, POOL_BUF, D_POOL))
        outs_s[1].append(sconv_s.reshape(dec_batch, SCONV_K - 1, D_SCONV))
        outs_s[2].append(cconv_s.reshape(dec_batch, CCONV_K - 1, D_CCONV))
        outs_s[3].append(jnp.concatenate([ffa0_s, ffg0_s, ffa1_s, ffg1_s], axis=-1)
                         .reshape(dec_batch, FFN_CONV_K - 1, 2 * D_FF))

    y_prompt = xp.reshape(batch, seq, D_MODEL)
    y_sample = xs.reshape(dec_batch, dec_seq, D_MODEL)
    return (y_prompt, y_sample,
            jnp.stack(outs_p[0]), jnp.stack(outs_p[1]), jnp.stack(outs_p[2]), jnp.stack(outs_p[3]),
            jnp.stack(outs_s[0]), jnp.stack(outs_s[1]), jnp.stack(outs_s[2]), jnp.stack(outs_s[3]))
```

```python
import functools

import jax
import jax.numpy as jnp
from jax import lax
from jax.experimental import pallas as pl
from jax.experimental.pallas import tpu as pltpu

EPS = 1e-6
D_MODEL = 2048
D_POOL = 512
D_SCONV = 768
D_CCONV = 768
POOL_WINDOWS = (2, 4, 8, 16)
POOL_GROUP = 128
POOL_BUF = 15
SCONV_K = 3
CCONV_K = 31
FFN_CONV_K = 3
D_FF = 5632
D_IN = D_POOL + 3 * D_SCONV + 2 * D_CCONV
O1 = D_POOL
O2 = O1 + D_SCONV
O3 = O2 + D_SCONV
O4 = O3 + D_SCONV
O5 = O4 + D_CCONV

SUBLANES = 8
LANES = 128
MXU_N = 256
VMEM_LIMIT_BYTES = 60 * 1024 * 1024

MIX_TM_PROMPT = 256
MIX_SAMPLE_SEQS = 32
FFN_TM_PROMPT = 1024
FFN_CK = 512
FFN_ROW_SPLITS = 4
CONV_ROW_BLOCK = 32

BF16 = jnp.bfloat16
F32 = jnp.float32


def _round_up(n, m):
    return (n + m - 1) // m * m


def _dot(a, b):
    return jnp.dot(a, b, preferred_element_type=F32)


def _rms_scale(x):
    return lax.rsqrt(jnp.mean(x * x, axis=-1, keepdims=True) + EPS)


def _rows(ref):
    v = ref[...]
    return v.reshape(-1, v.shape[-1])


def _store_rows(ref, v):
    ref[...] = v.reshape(ref.shape)


def _conv_aligned_stride(ext_ref, w_ref, b_ref, out_ref, ln, halo, tm, tstride):
    k = w_ref.shape[0]
    for rb in range(tm // CONV_ROW_BLOCK):
        r0 = rb * CONV_ROW_BLOCK
        acc = jnp.broadcast_to(b_ref[:, ln], (CONV_ROW_BLOCK, LANES))
        for t in range(k):
            start = halo - (k - 1 - t) * tstride + r0
            acc = acc + w_ref[t:t + 1, ln] * ext_ref[start:start + CONV_ROW_BLOCK, ln]
        out_ref[r0:r0 + CONV_ROW_BLOCK, ln] = acc


def _shift_rows(ext_ref, shift_ref, ln):
    row = lax.broadcasted_iota(jnp.int32, (SUBLANES, LANES), 0)
    prev = None
    for q in range(ext_ref.shape[0] // SUBLANES):
        t = ext_ref[q * SUBLANES:(q + 1) * SUBLANES, ln]
        rolled = [pltpu.roll(t, r, axis=0) for r in range(1, SUBLANES)]
        if prev is not None:
            for r in range(1, SUBLANES):
                shift_ref[r - 1, q * SUBLANES:(q + 1) * SUBLANES, ln] = jnp.where(
                    row >= r, rolled[r - 1], prev[r - 1])
        prev = rolled


def _conv_unit_stride(ext_ref, shift_ref, w_ref, b_ref, out_ref, ln, halo, tm):
    k = w_ref.shape[0]
    for rb in range(tm // CONV_ROW_BLOCK):
        r0 = rb * CONV_ROW_BLOCK
        acc = jnp.broadcast_to(b_ref[:, ln], (CONV_ROW_BLOCK, LANES))
        for d in range(k):
            a, r = divmod(d, SUBLANES)
            start = halo + r0 - a * SUBLANES
            src = ext_ref if r == 0 else shift_ref.at[r - 1]
            acc = acc + w_ref[k - 1 - d:k - d, ln] * src[start:start + CONV_ROW_BLOCK, ln]
        out_ref[r0:r0 + CONV_ROW_BLOCK, ln] = acc


def _mixer_body(*refs, tm, tstride, tiles_per_seq, has_state, round_ffn_weights):
    it = iter(refs)
    x_ref = next(it)
    if has_state:
        st_pool_ref, st_sconv_ref, st_cconv_ref = next(it), next(it), next(it)
    g1_ref, w_in_ref, pool_w_ref, pool_scale_ref = next(it), next(it), next(it), next(it)
    sconv_w_ref, cconv_w_ref, cconv_b_ref, cconv_g_ref, w_out_ref = (
        next(it), next(it), next(it), next(it), next(it))
    if round_ffn_weights:
        w_up_ref, w_down_ref = next(it), next(it)
    o_ref, pool_o_ref, sconv_o_ref, cconv_o_ref = next(it), next(it), next(it), next(it)
    if round_ffn_weights:
        w_up_o_ref, w_down_o_ref = next(it), next(it)
    h_ref, mix_ref, ext_p_ref, ext_s_ref, ext_c_ref, cc_ref, proj_s_ref = (
        next(it), next(it), next(it), next(it), next(it), next(it), next(it))
    if not has_state:
        shift_ref = next(it)

    s = tstride
    hp = ext_p_ref.shape[0] - tm
    hs = ext_s_ref.shape[0] - tm
    hc = ext_c_ref.shape[0] - tm
    i = pl.program_id(0)

    if has_state:
        ext_p_ref[0:hp, :] = _rows(st_pool_ref)
        ext_s_ref[0:hs, :] = _rows(st_sconv_ref)
        ext_c_ref[0:hc, :] = _rows(st_cconv_ref)
    else:
        @pl.when(i % tiles_per_seq == 0)
        def _():
            ext_p_ref[0:hp, :] = jnp.zeros((hp, D_POOL), F32)
            ext_s_ref[0:hs, :] = jnp.zeros((hs, D_SCONV), F32)
            ext_c_ref[0:hc, :] = jnp.zeros((hc, D_CCONV), F32)

    if round_ffn_weights:
        w_up_o_ref[...] = w_up_ref[...].astype(BF16)
        w_down_o_ref[...] = w_down_ref[...].astype(BF16)

    x = _rows(x_ref)
    h_ref[...] = (x * _rms_scale(x) * g1_ref[...]).astype(BF16)

    def in_proj(col0, c):
        return _dot(h_ref[...], w_in_ref[:, col0 + c * MXU_N:col0 + (c + 1) * MXU_N])

    for c in range(D_CCONV // MXU_N):
        ln = slice(c * MXU_N, (c + 1) * MXU_N)
        ext_c_ref[hc:hc + tm, ln] = in_proj(O4, c) * jax.nn.sigmoid(in_proj(O5, c))

    for c in range(3 * D_SCONV // MXU_N):
        proj_s_ref[:, c * MXU_N:(c + 1) * MXU_N] = in_proj(O1, c)
    for c in range(D_POOL // MXU_N):
        ext_p_ref[hp:hp + tm, c * MXU_N:(c + 1) * MXU_N] = in_proj(0, c)

    for lt in range(D_CCONV // LANES):
        ln = slice(lt * LANES, (lt + 1) * LANES)
        if s == 1:
            _shift_rows(ext_c_ref, shift_ref, ln)
            _conv_unit_stride(ext_c_ref, shift_ref, cconv_w_ref, cconv_b_ref, cc_ref, ln, hc, tm)
        else:
            _conv_aligned_stride(ext_c_ref, cconv_w_ref, cconv_b_ref, cc_ref, ln, hc, tm, s)
    cc = cc_ref[...]
    z = cc * _rms_scale(cc) * cconv_g_ref[...]
    mix_ref[:, O2:D_MODEL] = (z * jax.nn.sigmoid(z)).astype(BF16)

    gate_b = proj_s_ref[:, 0:D_SCONV]
    q = proj_s_ref[:, D_SCONV:2 * D_SCONV] * proj_s_ref[:, 2 * D_SCONV:3 * D_SCONV]
    ext_s_ref[hs:hs + tm, :] = q
    conv = sconv_w_ref[SCONV_K - 1:SCONV_K, :] * q
    for t in range(SCONV_K - 1):
        back = (SCONV_K - 1 - t) * s
        conv = conv + sconv_w_ref[t:t + 1, :] * ext_s_ref[hs - back:hs - back + tm, :]
    mix_ref[:, O1:O2] = (gate_b * conv).astype(BF16)

    if not has_state:
        pos = (i % tiles_per_seq) * tm + lax.broadcasted_iota(jnp.int32, (tm, 1), 0)
    for g, k in enumerate(POOL_WINDOWS):
        ln = slice(g * POOL_GROUP, (g + 1) * POOL_GROUP)
        u = ext_p_ref[hp:hp + tm, ln]
        win = u
        for m in range(1, k):
            win = win + ext_p_ref[hp - m * s:hp - m * s + tm, ln]
        if has_state:
            pooled = win / float(k)
        else:
            pooled = win / jnp.minimum(pos + 1, k).astype(F32)
        d = (pooled - u).astype(BF16)
        y = _dot(d, pool_w_ref[g]) * pool_scale_ref[:, ln]
        mix_ref[:, ln] = y.astype(BF16)

    _store_rows(o_ref, _rows(x_ref) + _dot(mix_ref[...], w_out_ref[...]))

    nb_p, nb_s, nb_c = POOL_BUF * s, (SCONV_K - 1) * s, (CCONV_K - 1) * s
    new_p = ext_p_ref[hp + tm - nb_p:hp + tm, :]
    new_s = ext_s_ref[hs + tm - nb_s:hs + tm, :]
    new_c = ext_c_ref[hc + tm - nb_c:hc + tm, :]
    if has_state:
        _store_rows(pool_o_ref, new_p)
        _store_rows(sconv_o_ref, new_s)
        _store_rows(cconv_o_ref, new_c)
    else:
        pool_o_ref[0] = new_p
        sconv_o_ref[0] = new_s
        cconv_o_ref[0] = new_c
        ext_p_ref[0:hp, :] = ext_p_ref[tm:tm + hp, :]
        ext_s_ref[0:hs, :] = ext_s_ref[tm:tm + hs, :]
        ext_c_ref[0:hc, :] = ext_c_ref[tm:tm + hc, :]


def _layer_spec(shape, layer, single_buffer=False):
    idx = lambda *_: (layer,) + (0,) * len(shape)
    if single_buffer:
        return pl.BlockSpec((None,) + shape, idx, pipeline_mode=pl.Buffered(1))
    return pl.BlockSpec((None,) + shape, idx)


def _mixer_call(x, states, w, layer, *, tm, tstride, tiles_per_seq, n_seq, ffn_weights=None):
    has_state = states is not None
    round_ffn_weights = ffn_weights is not None
    widths = [D_POOL, D_SCONV, D_CCONV]
    bufs = [POOL_BUF, SCONV_K - 1, CCONV_K - 1]
    if has_state:
        t_steps, n_b, _ = x.shape
        n_tiles = n_b // tstride
        halo = [nb * tstride for nb in bufs]
        x_spec = pl.BlockSpec((t_steps, tstride, D_MODEL), lambda i: (0, i, 0))
    else:
        n_tiles = x.shape[0] // tm
        halo = [_round_up(nb, SUBLANES) for nb in bufs]
        x_spec = pl.BlockSpec((tm, D_MODEL), lambda i: (i, 0))

    in_specs = [x_spec]
    args = [x]
    if has_state:
        for st, nb, wd in zip(states, bufs, widths):
            in_specs.append(pl.BlockSpec((None, nb, tstride, wd), lambda i: (layer, 0, i, 0)))
            args.append(st)
    in_specs += [
        _layer_spec((1, D_MODEL), layer),
        _layer_spec((D_MODEL, D_IN), layer, single_buffer=True),
        _layer_spec((len(POOL_WINDOWS), POOL_GROUP, POOL_GROUP), layer),
        _layer_spec((1, D_POOL), layer),
        _layer_spec((SCONV_K, D_SCONV), layer),
        _layer_spec((CCONV_K, D_CCONV), layer),
        _layer_spec((1, D_CCONV), layer),
        _layer_spec((1, D_CCONV), layer),
        _layer_spec((D_MODEL, D_MODEL), layer, single_buffer=True),
    ]
    args += [w["norm1_g"], w["w_in"], w["pool_w"], w["pool_scale"], w["sconv_w"],
             w["cconv_w"], w["cconv_b"], w["cconv_norm_g"], w["w_out"]]
    if round_ffn_weights:
        slabs = [(D_MODEL // n_tiles, 2 * D_FF), (D_FF // n_tiles, D_MODEL)]
        for wt, slab in zip(ffn_weights, slabs):
            in_specs.append(pl.BlockSpec((None,) + slab, lambda i: (layer, i, 0)))
            args.append(wt)

    out_shape = [jax.ShapeDtypeStruct(x.shape, F32)]
    out_specs = [x_spec]
    for nb, wd in zip(bufs, widths):
        if has_state:
            out_shape.append(jax.ShapeDtypeStruct((nb, n_b, wd), F32))
            out_specs.append(pl.BlockSpec((nb, tstride, wd), lambda i: (0, i, 0)))
        else:
            out_shape.append(jax.ShapeDtypeStruct((n_seq, nb, wd), F32))
            out_specs.append(pl.BlockSpec((1, nb, wd), lambda i: (i // tiles_per_seq, 0, 0)))

    if round_ffn_weights:
        for slab in slabs:
            out_shape.append(jax.ShapeDtypeStruct((slab[0] * n_tiles, slab[1]), BF16))
            out_specs.append(pl.BlockSpec(slab, lambda i: (i, 0)))

    scratch = [
        pltpu.VMEM((tm, D_MODEL), BF16),
        pltpu.VMEM((tm, D_MODEL), BF16),
        pltpu.VMEM((halo[0] + tm, D_POOL), F32),
        pltpu.VMEM((halo[1] + tm, D_SCONV), F32),
        pltpu.VMEM((halo[2] + tm, D_CCONV), F32),
        pltpu.VMEM((tm, D_CCONV), F32),
        pltpu.VMEM((tm, 3 * D_SCONV), F32),
    ]
    if not has_state:
        scratch.append(pltpu.VMEM((SUBLANES - 1, halo[2] + tm, D_CCONV), F32))
    body = functools.partial(_mixer_body, tm=tm, tstride=tstride, tiles_per_seq=tiles_per_seq,
                             has_state=has_state, round_ffn_weights=round_ffn_weights)
    return pl.pallas_call(
        body,
        grid=(n_tiles,),
        in_specs=in_specs,
        out_specs=out_specs,
        out_shape=out_shape,
        scratch_shapes=scratch,
        compiler_params=pltpu.CompilerParams(
            dimension_semantics=("arbitrary",), vmem_limit_bytes=VMEM_LIMIT_BYTES),
        name="mixer_state" if has_state else "mixer_prompt",
    )(*args)


def _row_block(ref, r0, rows):
    if len(ref.shape) == 2:
        return ref.at[r0:r0 + rows]
    n_b = ref.shape[1]
    return ref.at[r0 // n_b:(r0 + rows) // n_b]


def _ffn_body(*refs, tm, tstride, tiles_per_seq, has_state, final_norm):
    it = iter(refs)
    x_ref = next(it)
    if has_state:
        st_a_ref, st_g_ref = next(it), next(it)
    g2_ref, wa_ref, wg_ref, cwa_ref, cwg_ref, wd_ref = (
        next(it), next(it), next(it), next(it), next(it), next(it))
    if final_norm:
        gf_ref = next(it)
    o_ref, new_a_ref, new_g_ref = next(it), next(it), next(it)
    h_ref, ext_ref = next(it), next(it)
    if not has_state:
        halo_ref = next(it)

    s = tstride
    ck = wa_ref.shape[1]
    hl = ext_ref.shape[0] - tm
    rows = tm // FFN_ROW_SPLITS
    i = pl.program_id(0)
    j = pl.program_id(1)
    nj = pl.num_programs(1)

    @pl.when(j == 0)
    def _():
        x = _rows(x_ref)
        h_ref[...] = (x * _rms_scale(x) * g2_ref[...]).astype(BF16)
        _store_rows(o_ref, x)
        if not has_state:
            @pl.when(i % tiles_per_seq == 0)
            def _():
                halo_ref[...] = jnp.zeros(halo_ref.shape, F32)

    if has_state:
        ext_ref[0:hl, 0:ck] = _rows(st_a_ref)
        ext_ref[0:hl, ck:2 * ck] = _rows(st_g_ref)
    else:
        ext_ref[0:hl, :] = halo_ref[j]

    w_a, w_g, w_d = wa_ref[...], wg_ref[...], wd_ref[...]
    for r in range(FFN_ROW_SPLITS):
        r0 = r * rows
        h = h_ref[r0:r0 + rows, :]
        ext_ref[hl + r0:hl + r0 + rows, 0:ck] = _dot(h, w_a)
        ext_ref[hl + r0:hl + r0 + rows, ck:2 * ck] = _dot(h, w_g)

    def conv(cw_ref, r0, lo):
        out = None
        for t in range(FFN_CONV_K):
            start = hl + r0 - (FFN_CONV_K - 1 - t) * s
            term = cw_ref[t:t + 1, :] * ext_ref[start:start + rows, lo:lo + ck]
            out = term if out is None else out + term
        return out

    for r in range(FFN_ROW_SPLITS):
        r0 = r * rows
        c_a = conv(cwa_ref, r0, 0)
        c_g = conv(cwg_ref, r0, ck)
        act = (c_a * jax.nn.sigmoid(c_a) * c_g).astype(BF16)
        o_blk = _row_block(o_ref, r0, rows)
        _store_rows(o_blk, _rows(o_blk) + _dot(act, w_d))

    nb = (FFN_CONV_K - 1) * s
    new_a = ext_ref[hl + tm - nb:hl + tm, 0:ck]
    new_g = ext_ref[hl + tm - nb:hl + tm, ck:2 * ck]
    if has_state:
        _store_rows(new_a_ref, new_a)
        _store_rows(new_g_ref, new_g)
    else:
        seq = i // tiles_per_seq
        new_a_ref[seq, j] = new_a
        new_g_ref[seq, j] = new_g
        halo_ref[j] = ext_ref[tm:tm + hl, :]

    if final_norm:
        @pl.when(j == nj - 1)
        def _():
            y = _rows(o_ref)
            _store_rows(o_ref, y * _rms_scale(y) * gf_ref[...])


def _ffn_call(x, state, w, w_up_b, w_down_b, layer, final_g, *, tm, tstride, tiles_per_seq, n_seq):
    has_state = state is not None
    final_norm = final_g is not None
    ck = FFN_CK
    nj = D_FF // ck
    nb = FFN_CONV_K - 1
    if has_state:
        t_steps, n_b, _ = x.shape
        n_tiles = n_b // tstride
        hl = nb * tstride
        x_block, x_index = (t_steps, tstride, D_MODEL), lambda i, j: (0, i, 0)
    else:
        n_tiles = x.shape[0] // tm
        hl = _round_up(nb, SUBLANES)
        x_block, x_index = (tm, D_MODEL), lambda i, j: (i, 0)

    in_specs = [pl.BlockSpec(x_block, x_index)]
    args = [x]
    if has_state:
        in_specs += [pl.BlockSpec((None, nb, tstride, ck), lambda i, j: (layer, 0, i, j)),
                     pl.BlockSpec((None, nb, tstride, ck), lambda i, j: (layer, 0, i, nj + j))]
        args += [state, state]
    in_specs += [
        _layer_spec((1, D_MODEL), layer),
        pl.BlockSpec((D_MODEL, ck), lambda i, j: (0, j)),
        pl.BlockSpec((D_MODEL, ck), lambda i, j: (0, nj + j)),
        pl.BlockSpec((None, FFN_CONV_K, ck), lambda i, j: (layer, 0, j)),
        pl.BlockSpec((None, FFN_CONV_K, ck), lambda i, j: (layer, 0, nj + j)),
        pl.BlockSpec((ck, D_MODEL), lambda i, j: (j, 0)),
    ]
    args += [w["norm2_g"], w_up_b, w_up_b, w["ffn_conv_w"], w["ffn_conv_w"], w_down_b]
    if final_norm:
        in_specs.append(pl.BlockSpec((1, D_MODEL), lambda i, j: (0, 0)))
        args.append(final_g)

    out_shape = [jax.ShapeDtypeStruct(x.shape, F32)]
    out_specs = [pl.BlockSpec(x_block, x_index)]
    for _ in range(2):
        if has_state:
            out_shape.append(jax.ShapeDtypeStruct((nb, n_b, D_FF), F32))
            out_specs.append(pl.BlockSpec((nb, tstride, ck), lambda i, j: (0, i, j)))
        else:
            out_shape.append(jax.ShapeDtypeStruct((n_seq, nj, nb, ck), F32))
            out_specs.append(pl.BlockSpec((n_seq, nj, nb, ck), lambda i, j: (0, 0, 0, 0)))

    scratch = [pltpu.VMEM((tm, D_MODEL), BF16), pltpu.VMEM((hl + tm, 2 * ck), F32)]
    if not has_state:
        scratch.append(pltpu.VMEM((nj, hl, 2 * ck), F32))
    body = functools.partial(_ffn_body, tm=tm, tstride=tstride, tiles_per_seq=tiles_per_seq,
                             has_state=has_state, final_norm=final_norm)
    return pl.pallas_call(
        body,
        grid=(n_tiles, nj),
        in_specs=in_specs,
        out_specs=out_specs,
        out_shape=out_shape,
        scratch_shapes=scratch,
        compiler_params=pltpu.CompilerParams(
            dimension_semantics=("arbitrary", "arbitrary"), vmem_limit_bytes=VMEM_LIMIT_BYTES),
        name="ffn_state" if has_state else "ffn_prompt",
    )(*args)


def _time_major(a):
    return jnp.swapaxes(a, -3, -2)


def kernel(x_prompt, x_sample, state_pool, state_sconv, state_cconv, state_ffn, norm1_g, w_in,
           pool_w, pool_scale, sconv_w, cconv_w, cconv_b, cconv_norm_g, w_out, norm2_g, w_up,
           ffn_conv_w, w_down, final_norm_g):
    batch, seq, _ = x_prompt.shape
    dec_batch, dec_seq, _ = x_sample.shape
    depth = w_in.shape[0]

    w = {
        "norm1_g": norm1_g.reshape(depth, 1, D_MODEL),
        "w_in": w_in.astype(BF16),
        "pool_w": pool_w.astype(BF16),
        "pool_scale": pool_scale.reshape(depth, 1, D_POOL),
        "sconv_w": sconv_w,
        "cconv_w": cconv_w,
        "cconv_b": cconv_b.reshape(depth, 1, D_CCONV),
        "cconv_norm_g": cconv_norm_g.reshape(depth, 1, D_CCONV),
        "w_out": w_out.astype(BF16),
        "norm2_g": norm2_g.reshape(depth, 1, D_MODEL),
        "ffn_conv_w": ffn_conv_w,
    }
    final_g = final_norm_g.reshape(1, D_MODEL)

    xp = x_prompt.reshape(batch * seq, D_MODEL)
    xs = _time_major(x_sample)
    st_mix = [_time_major(state_pool), _time_major(state_sconv), _time_major(state_cconv)]
    st_ffn = _time_major(state_ffn)

    outs_p = [[], [], [], []]
    outs_s = [[], [], [], []]
    for l in range(depth):
        last = final_g if l == depth - 1 else None

        xp, pool_p, sconv_p, cconv_p, w_up_b, w_down_b = _mixer_call(
            xp, None, w, l, tm=MIX_TM_PROMPT, tstride=1,
            tiles_per_seq=seq // MIX_TM_PROMPT, n_seq=batch, ffn_weights=(w_up, w_down))
        xp, ffa_p, ffg_p = _ffn_call(
            xp, None, w, w_up_b, w_down_b, l, last, tm=FFN_TM_PROMPT, tstride=1,
            tiles_per_seq=seq // FFN_TM_PROMPT, n_seq=batch)
        unchunk = lambda a: a.transpose(0, 2, 1, 3).reshape(batch, FFN_CONV_K - 1, D_FF)
        outs_p[0].append(pool_p)
        outs_p[1].append(sconv_p)
        outs_p[2].append(cconv_p)
        outs_p[3].append(jnp.concatenate([unchunk(ffa_p), unchunk(ffg_p)], axis=-1))

        xs, pool_s, sconv_s, cconv_s = _mixer_call(
            xs, st_mix, w, l, tm=dec_seq * MIX_SAMPLE_SEQS, tstride=MIX_SAMPLE_SEQS,
            tiles_per_seq=1, n_seq=dec_batch)
        xs, ffa_s, ffg_s = _ffn_call(
            xs, st_ffn, w, w_up_b, w_down_b, l, last, tm=dec_seq * dec_batch, tstride=dec_batch,
            tiles_per_seq=1, n_seq=dec_batch)
        outs_s[0].append(pool_s)
        outs_s[1].append(sconv_s)
        outs_s[2].append(cconv_s)
        outs_s[3].append(jnp.concatenate([ffa_s, ffg_s], axis=-1))

    y_prompt = xp.reshape(batch, seq, D_MODEL)
    y_sample = _time_major(xs)
    stack_tm = lambda parts: _time_major(jnp.stack(parts))
    return (y_prompt, y_sample,
            jnp.stack(outs_p[0]), jnp.stack(outs_p[1]), jnp.stack(outs_p[2]), jnp.stack(outs_p[3]),
            stack_tm(outs_s[0]), stack_tm(outs_s[1]), stack_tm(outs_s[2]), stack_tm(outs_s[3]))
```

```python
import functools

import jax
import jax.numpy as jnp
from jax import lax
from jax.experimental import pallas as pl
from jax.experimental.pallas import tpu as pltpu

EPS = 1e-6
D_MODEL = 2048
D_POOL = 512
D_SCONV = 768
D_CCONV = 768
POOL_WINDOWS = (2, 4, 8, 16)
POOL_GROUP = 128
POOL_BUF = 15
SCONV_K = 3
CCONV_K = 31
FFN_CONV_K = 3
D_FF = 5632
D_IN = D_POOL + 3 * D_SCONV + 2 * D_CCONV
O1 = D_POOL
O2 = O1 + D_SCONV
O3 = O2 + D_SCONV
O4 = O3 + D_SCONV
O5 = O4 + D_CCONV

SUBLANES = 8
LANES = 128
MXU_N = 256
VMEM_LIMIT_BYTES = 60 * 1024 * 1024

MIX_TM_PROMPT = 256
MIX_SAMPLE_SEQS = 32
FFN_TM_PROMPT = 1024
FFN_CK = 512
FFN_ROW_SPLITS = 4
CONV_ROW_BLOCK = 32

BF16 = jnp.bfloat16
F32 = jnp.float32


def _round_up(n, m):
    return (n + m - 1) // m * m


def _dot(a, b):
    return jnp.dot(a, b, preferred_element_type=F32)


def _rms_scale(x):
    return lax.rsqrt(jnp.mean(x * x, axis=-1, keepdims=True) + EPS)


def _rows(ref):
    v = ref[...]
    return v.reshape(-1, v.shape[-1])


def _store_rows(ref, v):
    ref[...] = v.reshape(ref.shape)


def _conv_aligned_stride(ext_ref, w_ref, b_ref, out_ref, ln, halo, tm, tstride):
    k = w_ref.shape[0]
    for rb in range(tm // CONV_ROW_BLOCK):
        r0 = rb * CONV_ROW_BLOCK
        acc = jnp.broadcast_to(b_ref[:, ln], (CONV_ROW_BLOCK, LANES))
        for t in range(k):
            start = halo - (k - 1 - t) * tstride + r0
            acc = acc + w_ref[t:t + 1, ln] * ext_ref[start:start + CONV_ROW_BLOCK, ln]
        out_ref[r0:r0 + CONV_ROW_BLOCK, ln] = acc


def _shift_rows(ext_ref, shift_ref, ln):
    row = lax.broadcasted_iota(jnp.int32, (SUBLANES, LANES), 0)
    prev = None
    for q in range(ext_ref.shape[0] // SUBLANES):
        t = ext_ref[q * SUBLANES:(q + 1) * SUBLANES, ln]
        rolled = [pltpu.roll(t, r, axis=0) for r in range(1, SUBLANES)]
        if prev is not None:
            for r in range(1, SUBLANES):
                shift_ref[r - 1, q * SUBLANES:(q + 1) * SUBLANES, ln] = jnp.where(
                    row >= r, rolled[r - 1], prev[r - 1])
        prev = rolled


def _conv_unit_stride(ext_ref, shift_ref, w_ref, b_ref, out_ref, ln, halo, tm):
    k = w_ref.shape[0]
    for rb in range(tm // CONV_ROW_BLOCK):
        r0 = rb * CONV_ROW_BLOCK
        acc = jnp.broadcast_to(b_ref[:, ln], (CONV_ROW_BLOCK, LANES))
        for d in range(k):
            a, r = divmod(d, SUBLANES)
            start = halo + r0 - a * SUBLANES
            src = ext_ref if r == 0 else shift_ref.at[r - 1]
            acc = acc + w_ref[k - 1 - d:k - d, ln] * src[start:start + CONV_ROW_BLOCK, ln]
        out_ref[r0:r0 + CONV_ROW_BLOCK, ln] = acc


def _mixer_body(*refs, tm, tstride, tiles_per_seq, has_state, n_round):
    it = iter(refs)
    x_ref = next(it)
    if has_state:
        st_pool_ref, st_sconv_ref, st_cconv_ref = next(it), next(it), next(it)
    g1_ref, w_in_ref, pool_w_ref, pool_scale_ref = next(it), next(it), next(it), next(it)
    sconv_w_ref, cconv_w_ref, cconv_b_ref, cconv_g_ref, w_out_ref = (
        next(it), next(it), next(it), next(it), next(it))
    round_in_refs = [next(it) for _ in range(n_round)]
    o_ref, pool_o_ref, sconv_o_ref, cconv_o_ref = next(it), next(it), next(it), next(it)
    round_out_refs = [next(it) for _ in range(n_round)]
    h_ref, mix_ref, ext_p_ref, ext_s_ref, ext_c_ref, cc_ref, proj_s_ref = (
        next(it), next(it), next(it), next(it), next(it), next(it), next(it))
    if not has_state:
        shift_ref = next(it)

    s = tstride
    hp = ext_p_ref.shape[0] - tm
    hs = ext_s_ref.shape[0] - tm
    hc = ext_c_ref.shape[0] - tm
    i = pl.program_id(0)

    if has_state:
        ext_p_ref[0:hp, :] = _rows(st_pool_ref)
        ext_s_ref[0:hs, :] = _rows(st_sconv_ref)
        ext_c_ref[0:hc, :] = _rows(st_cconv_ref)
    else:
        @pl.when(i % tiles_per_seq == 0)
        def _():
            ext_p_ref[0:hp, :] = jnp.zeros((hp, D_POOL), F32)
            ext_s_ref[0:hs, :] = jnp.zeros((hs, D_SCONV), F32)
            ext_c_ref[0:hc, :] = jnp.zeros((hc, D_CCONV), F32)

    for src_ref, dst_ref in zip(round_in_refs, round_out_refs):
        dst_ref[...] = src_ref[...].astype(BF16)

    x = _rows(x_ref)
    h_ref[...] = (x * _rms_scale(x) * g1_ref[...]).astype(BF16)

    def in_proj(col0, c):
        return _dot(h_ref[...], w_in_ref[:, col0 + c * MXU_N:col0 + (c + 1) * MXU_N])

    for c in range(D_CCONV // MXU_N):
        ln = slice(c * MXU_N, (c + 1) * MXU_N)
        ext_c_ref[hc:hc + tm, ln] = in_proj(O4, c) * jax.nn.sigmoid(in_proj(O5, c))

    for c in range(3 * D_SCONV // MXU_N):
        proj_s_ref[:, c * MXU_N:(c + 1) * MXU_N] = in_proj(O1, c)
    for c in range(D_POOL // MXU_N):
        ext_p_ref[hp:hp + tm, c * MXU_N:(c + 1) * MXU_N] = in_proj(0, c)

    for lt in range(D_CCONV // LANES):
        ln = slice(lt * LANES, (lt + 1) * LANES)
        if s == 1:
            _shift_rows(ext_c_ref, shift_ref, ln)
            _conv_unit_stride(ext_c_ref, shift_ref, cconv_w_ref, cconv_b_ref, cc_ref, ln, hc, tm)
        else:
            _conv_aligned_stride(ext_c_ref, cconv_w_ref, cconv_b_ref, cc_ref, ln, hc, tm, s)
    cc = cc_ref[...]
    z = cc * _rms_scale(cc) * cconv_g_ref[...]
    mix_ref[:, O2:D_MODEL] = (z * jax.nn.sigmoid(z)).astype(BF16)

    gate_b = proj_s_ref[:, 0:D_SCONV]
    q = proj_s_ref[:, D_SCONV:2 * D_SCONV] * proj_s_ref[:, 2 * D_SCONV:3 * D_SCONV]
    ext_s_ref[hs:hs + tm, :] = q
    conv = sconv_w_ref[SCONV_K - 1:SCONV_K, :] * q
    for t in range(SCONV_K - 1):
        back = (SCONV_K - 1 - t) * s
        conv = conv + sconv_w_ref[t:t + 1, :] * ext_s_ref[hs - back:hs - back + tm, :]
    mix_ref[:, O1:O2] = (gate_b * conv).astype(BF16)

    if not has_state:
        pos = (i % tiles_per_seq) * tm + lax.broadcasted_iota(jnp.int32, (tm, 1), 0)
    for g, k in enumerate(POOL_WINDOWS):
        ln = slice(g * POOL_GROUP, (g + 1) * POOL_GROUP)
        u = ext_p_ref[hp:hp + tm, ln]
        win = u
        for m in range(1, k):
            win = win + ext_p_ref[hp - m * s:hp - m * s + tm, ln]
        if has_state:
            pooled = win / float(k)
        else:
            pooled = win / jnp.minimum(pos + 1, k).astype(F32)
        d = (pooled - u).astype(BF16)
        y = _dot(d, pool_w_ref[g]) * pool_scale_ref[:, ln]
        mix_ref[:, ln] = y.astype(BF16)

    _store_rows(o_ref, _rows(x_ref) + _dot(mix_ref[...], w_out_ref[...]))

    nb_p, nb_s, nb_c = POOL_BUF * s, (SCONV_K - 1) * s, (CCONV_K - 1) * s
    new_p = ext_p_ref[hp + tm - nb_p:hp + tm, :]
    new_s = ext_s_ref[hs + tm - nb_s:hs + tm, :]
    new_c = ext_c_ref[hc + tm - nb_c:hc + tm, :]
    if has_state:
        _store_rows(pool_o_ref, new_p)
        _store_rows(sconv_o_ref, new_s)
        _store_rows(cconv_o_ref, new_c)
    else:
        pool_o_ref[0] = new_p
        sconv_o_ref[0] = new_s
        cconv_o_ref[0] = new_c
        ext_p_ref[0:hp, :] = ext_p_ref[tm:tm + hp, :]
        ext_s_ref[0:hs, :] = ext_s_ref[tm:tm + hs, :]
        ext_c_ref[0:hc, :] = ext_c_ref[tm:tm + hc, :]


def _layer_spec(shape, layer, single_buffer=False):
    idx = lambda *_: (layer,) + (0,) * len(shape)
    if single_buffer:
        return pl.BlockSpec((None,) + shape, idx, pipeline_mode=pl.Buffered(1))
    return pl.BlockSpec((None,) + shape, idx)


def _mixer_call(x, states, w, layer, w_in_b, w_out_b, *, tm, tstride, tiles_per_seq, n_seq,
                round_weights=()):
    has_state = states is not None
    widths = [D_POOL, D_SCONV, D_CCONV]
    bufs = [POOL_BUF, SCONV_K - 1, CCONV_K - 1]
    if has_state:
        t_steps, n_b, _ = x.shape
        n_tiles = n_b // tstride
        halo = [nb * tstride for nb in bufs]
        x_spec = pl.BlockSpec((t_steps, tstride, D_MODEL), lambda i: (0, i, 0))
    else:
        n_tiles = x.shape[0] // tm
        halo = [_round_up(nb, SUBLANES) for nb in bufs]
        x_spec = pl.BlockSpec((tm, D_MODEL), lambda i: (i, 0))

    in_specs = [x_spec]
    args = [x]
    if has_state:
        for st, nb, wd in zip(states, bufs, widths):
            in_specs.append(pl.BlockSpec((None, nb, tstride, wd), lambda i: (layer, 0, i, 0)))
            args.append(st)
    in_specs += [
        _layer_spec((1, D_MODEL), layer),
        _layer_spec((D_MODEL, D_IN), 0, single_buffer=True),
        _layer_spec((len(POOL_WINDOWS), POOL_GROUP, POOL_GROUP), layer),
        _layer_spec((1, D_POOL), layer),
        _layer_spec((SCONV_K, D_SCONV), layer),
        _layer_spec((CCONV_K, D_CCONV), layer),
        _layer_spec((1, D_CCONV), layer),
        _layer_spec((1, D_CCONV), layer),
        _layer_spec((D_MODEL, D_MODEL), 0, single_buffer=True),
    ]
    args += [w["norm1_g"], w_in_b, w["pool_w"], w["pool_scale"], w["sconv_w"],
             w["cconv_w"], w["cconv_b"], w["cconv_norm_g"], w_out_b]
    slabs = [(wt.shape[1] // n_tiles, wt.shape[2]) for wt, _ in round_weights]
    for (wt, wl), slab in zip(round_weights, slabs):
        in_specs.append(pl.BlockSpec((None,) + slab, lambda i, wl=wl: (wl, i, 0)))
        args.append(wt)

    out_shape = [jax.ShapeDtypeStruct(x.shape, F32)]
    out_specs = [x_spec]
    for nb, wd in zip(bufs, widths):
        if has_state:
            out_shape.append(jax.ShapeDtypeStruct((nb, n_b, wd), F32))
            out_specs.append(pl.BlockSpec((nb, tstride, wd), lambda i: (0, i, 0)))
        else:
            out_shape.append(jax.ShapeDtypeStruct((n_seq, nb, wd), F32))
            out_specs.append(pl.BlockSpec((1, nb, wd), lambda i: (i // tiles_per_seq, 0, 0)))

    for slab in slabs:
        out_shape.append(jax.ShapeDtypeStruct((slab[0] * n_tiles, slab[1]), BF16))
        out_specs.append(pl.BlockSpec(slab, lambda i: (i, 0)))

    scratch = [
        pltpu.VMEM((tm, D_MODEL), BF16),
        pltpu.VMEM((tm, D_MODEL), BF16),
        pltpu.VMEM((halo[0] + tm, D_POOL), F32),
        pltpu.VMEM((halo[1] + tm, D_SCONV), F32),
        pltpu.VMEM((halo[2] + tm, D_CCONV), F32),
        pltpu.VMEM((tm, D_CCONV), F32),
        pltpu.VMEM((tm, 3 * D_SCONV), F32),
    ]
    if not has_state:
        scratch.append(pltpu.VMEM((SUBLANES - 1, halo[2] + tm, D_CCONV), F32))
    body = functools.partial(_mixer_body, tm=tm, tstride=tstride, tiles_per_seq=tiles_per_seq,
                             has_state=has_state, n_round=len(round_weights))
    return pl.pallas_call(
        body,
        grid=(n_tiles,),
        in_specs=in_specs,
        out_specs=out_specs,
        out_shape=out_shape,
        scratch_shapes=scratch,
        compiler_params=pltpu.CompilerParams(
            dimension_semantics=("arbitrary",), vmem_limit_bytes=VMEM_LIMIT_BYTES),
        name="mixer_state" if has_state else "mixer_prompt",
    )(*args)


def _row_block(ref, r0, rows):
    if len(ref.shape) == 2:
        return ref.at[r0:r0 + rows]
    n_b = ref.shape[1]
    return ref.at[r0 // n_b:(r0 + rows) // n_b]


def _ffn_body(*refs, tm, tstride, tiles_per_seq, has_state, final_norm):
    it = iter(refs)
    x_ref = next(it)
    if has_state:
        st_a_ref, st_g_ref = next(it), next(it)
    g2_ref, wa_ref, wg_ref, cwa_ref, cwg_ref, wd_ref = (
        next(it), next(it), next(it), next(it), next(it), next(it))
    if final_norm:
        gf_ref = next(it)
    o_ref, new_a_ref, new_g_ref = next(it), next(it), next(it)
    h_ref, ext_ref = next(it), next(it)
    if not has_state:
        halo_ref = next(it)

    s = tstride
    ck = wa_ref.shape[1]
    hl = ext_ref.shape[0] - tm
    rows = tm // FFN_ROW_SPLITS
    i = pl.program_id(0)
    j = pl.program_id(1)
    nj = pl.num_programs(1)

    @pl.when(j == 0)
    def _():
        x = _rows(x_ref)
        h_ref[...] = (x * _rms_scale(x) * g2_ref[...]).astype(BF16)
        _store_rows(o_ref, x)
        if not has_state:
            @pl.when(i % tiles_per_seq == 0)
            def _():
                halo_ref[...] = jnp.zeros(halo_ref.shape, F32)

    if has_state:
        ext_ref[0:hl, 0:ck] = _rows(st_a_ref)
        ext_ref[0:hl, ck:2 * ck] = _rows(st_g_ref)
    else:
        ext_ref[0:hl, :] = halo_ref[j]

    w_a, w_g, w_d = wa_ref[...], wg_ref[...], wd_ref[...]
    for r in range(FFN_ROW_SPLITS):
        r0 = r * rows
        h = h_ref[r0:r0 + rows, :]
        ext_ref[hl + r0:hl + r0 + rows, 0:ck] = _dot(h, w_a)
        ext_ref[hl + r0:hl + r0 + rows, ck:2 * ck] = _dot(h, w_g)

    def conv(cw_ref, r0, lo):
        out = None
        for t in range(FFN_CONV_K):
            start = hl + r0 - (FFN_CONV_K - 1 - t) * s
            term = cw_ref[t:t + 1, :] * ext_ref[start:start + rows, lo:lo + ck]
            out = term if out is None else out + term
        return out

    for r in range(FFN_ROW_SPLITS):
        r0 = r * rows
        c_a = conv(cwa_ref, r0, 0)
        c_g = conv(cwg_ref, r0, ck)
        act = (c_a * jax.nn.sigmoid(c_a) * c_g).astype(BF16)
        o_blk = _row_block(o_ref, r0, rows)
        _store_rows(o_blk, _rows(o_blk) + _dot(act, w_d))

    nb = (FFN_CONV_K - 1) * s
    new_a = ext_ref[hl + tm - nb:hl + tm, 0:ck]
    new_g = ext_ref[hl + tm - nb:hl + tm, ck:2 * ck]
    if has_state:
        _store_rows(new_a_ref, new_a)
        _store_rows(new_g_ref, new_g)
    else:
        seq = i // tiles_per_seq
        new_a_ref[seq, j] = new_a
        new_g_ref[seq, j] = new_g
        halo_ref[j] = ext_ref[tm:tm + hl, :]

    if final_norm:
        @pl.when(j == nj - 1)
        def _():
            y = _rows(o_ref)
            _store_rows(o_ref, y * _rms_scale(y) * gf_ref[...])


def _ffn_call(x, state, w, w_up_b, w_down_b, layer, final_g, *, tm, tstride, tiles_per_seq, n_seq):
    has_state = state is not None
    final_norm = final_g is not None
    ck = FFN_CK
    nj = D_FF // ck
    nb = FFN_CONV_K - 1
    if has_state:
        t_steps, n_b, _ = x.shape
        n_tiles = n_b // tstride
        hl = nb * tstride
        x_block, x_index = (t_steps, tstride, D_MODEL), lambda i, j: (0, i, 0)
    else:
        n_tiles = x.shape[0] // tm
        hl = _round_up(nb, SUBLANES)
        x_block, x_index = (tm, D_MODEL), lambda i, j: (i, 0)

    in_specs = [pl.BlockSpec(x_block, x_index)]
    args = [x]
    if has_state:
        in_specs += [pl.BlockSpec((None, nb, tstride, ck), lambda i, j: (layer, 0, i, j)),
                     pl.BlockSpec((None, nb, tstride, ck), lambda i, j: (layer, 0, i, nj + j))]
        args += [state, state]
    in_specs += [
        _layer_spec((1, D_MODEL), layer),
        pl.BlockSpec((D_MODEL, ck), lambda i, j: (0, j)),
        pl.BlockSpec((D_MODEL, ck), lambda i, j: (0, nj + j)),
        pl.BlockSpec((None, FFN_CONV_K, ck), lambda i, j: (layer, 0, j)),
        pl.BlockSpec((None, FFN_CONV_K, ck), lambda i, j: (layer, 0, nj + j)),
        pl.BlockSpec((ck, D_MODEL), lambda i, j: (j, 0)),
    ]
    args += [w["norm2_g"], w_up_b, w_up_b, w["ffn_conv_w"], w["ffn_conv_w"], w_down_b]
    if final_norm:
        in_specs.append(pl.BlockSpec((1, D_MODEL), lambda i, j: (0, 0)))
        args.append(final_g)

    out_shape = [jax.ShapeDtypeStruct(x.shape, F32)]
    out_specs = [pl.BlockSpec(x_block, x_index)]
    for _ in range(2):
        if has_state:
            out_shape.append(jax.ShapeDtypeStruct((nb, n_b, D_FF), F32))
            out_specs.append(pl.BlockSpec((nb, tstride, ck), lambda i, j: (0, i, j)))
        else:
            out_shape.append(jax.ShapeDtypeStruct((n_seq, nj, nb, ck), F32))
            out_specs.append(pl.BlockSpec((n_seq, nj, nb, ck), lambda i, j: (0, 0, 0, 0)))

    scratch = [pltpu.VMEM((tm, D_MODEL), BF16), pltpu.VMEM((hl + tm, 2 * ck), F32)]
    if not has_state:
        scratch.append(pltpu.VMEM((nj, hl, 2 * ck), F32))
    body = functools.partial(_ffn_body, tm=tm, tstride=tstride, tiles_per_seq=tiles_per_seq,
                             has_state=has_state, final_norm=final_norm)
    return pl.pallas_call(
        body,
        grid=(n_tiles, nj),
        in_specs=in_specs,
        out_specs=out_specs,
        out_shape=out_shape,
        scratch_shapes=scratch,
        compiler_params=pltpu.CompilerParams(
            dimension_semantics=("arbitrary", "arbitrary"), vmem_limit_bytes=VMEM_LIMIT_BYTES),
        name="ffn_state" if has_state else "ffn_prompt",
    )(*args)


def _time_major(a):
    return jnp.swapaxes(a, -3, -2)


def kernel(x_prompt, x_sample, state_pool, state_sconv, state_cconv, state_ffn, norm1_g, w_in,
           pool_w, pool_scale, sconv_w, cconv_w, cconv_b, cconv_norm_g, w_out, norm2_g, w_up,
           ffn_conv_w, w_down, final_norm_g):
    batch, seq, _ = x_prompt.shape
    dec_batch, dec_seq, _ = x_sample.shape
    depth = w_in.shape[0]

    w = {
        "norm1_g": norm1_g.reshape(depth, 1, D_MODEL),
        "pool_w": pool_w.astype(BF16),
        "pool_scale": pool_scale.reshape(depth, 1, D_POOL),
        "sconv_w": sconv_w,
        "cconv_w": cconv_w,
        "cconv_b": cconv_b.reshape(depth, 1, D_CCONV),
        "cconv_norm_g": cconv_norm_g.reshape(depth, 1, D_CCONV),
        "norm2_g": norm2_g.reshape(depth, 1, D_MODEL),
        "ffn_conv_w": ffn_conv_w,
    }
    final_g = final_norm_g.reshape(1, D_MODEL)

    xp = x_prompt.reshape(batch * seq, D_MODEL)
    xs = _time_major(x_sample)
    st_mix = [_time_major(state_pool), _time_major(state_sconv), _time_major(state_cconv)]
    st_ffn = _time_major(state_ffn)

    w_in_b, w_out_b = w_in[:1].astype(BF16), w_out[:1].astype(BF16)

    outs_p = [[], [], [], []]
    outs_s = [[], [], [], []]
    for l in range(depth):
        last = final_g if l == depth - 1 else None
        rounding = [(w_up, l), (w_down, l)]
        if l + 1 < depth:
            rounding += [(w_in, l + 1), (w_out, l + 1)]

        xp, pool_p, sconv_p, cconv_p, *rounded = _mixer_call(
            xp, None, w, l, w_in_b, w_out_b, tm=MIX_TM_PROMPT, tstride=1,
            tiles_per_seq=seq // MIX_TM_PROMPT, n_seq=batch, round_weights=rounding)
        w_up_b, w_down_b = rounded[:2]
        xp, ffa_p, ffg_p = _ffn_call(
            xp, None, w, w_up_b, w_down_b, l, last, tm=FFN_TM_PROMPT, tstride=1,
            tiles_per_seq=seq // FFN_TM_PROMPT, n_seq=batch)
        unchunk = lambda a: a.transpose(0, 2, 1, 3).reshape(batch, FFN_CONV_K - 1, D_FF)
        outs_p[0].append(pool_p)
        outs_p[1].append(sconv_p)
        outs_p[2].append(cconv_p)
        outs_p[3].append(jnp.concatenate([unchunk(ffa_p), unchunk(ffg_p)], axis=-1))

        xs, pool_s, sconv_s, cconv_s = _mixer_call(
            xs, st_mix, w, l, w_in_b, w_out_b, tm=dec_seq * MIX_SAMPLE_SEQS,
            tstride=MIX_SAMPLE_SEQS, tiles_per_seq=1, n_seq=dec_batch)
        xs, ffa_s, ffg_s = _ffn_call(
            xs, st_ffn, w, w_up_b, w_down_b, l, last, tm=dec_seq * dec_batch, tstride=dec_batch,
            tiles_per_seq=1, n_seq=dec_batch)
        outs_s[0].append(pool_s)
        outs_s[1].append(sconv_s)
        outs_s[2].append(cconv_s)
        outs_s[3].append(jnp.concatenate([ffa_s, ffg_s], axis=-1))
        if l + 1 < depth:
            w_in_b, w_out_b = rounded[2][None], rounded[3][None]

    y_prompt = xp.reshape(batch, seq, D_MODEL)
    y_sample = _time_major(xs)
    stack_tm = lambda parts: _time_major(jnp.stack(parts))
    return (y_prompt, y_sample,
            jnp.stack(outs_p[0]), jnp.stack(outs_p[1]), jnp.stack(outs_p[2]), jnp.stack(outs_p[3]),
            stack_tm(outs_s[0]), stack_tm(outs_s[1]), stack_tm(outs_s[2]), stack_tm(outs_s[3]))
```

```python
import functools

import jax
import jax.numpy as jnp
from jax import lax
from jax.experimental import pallas as pl
from jax.experimental.pallas import tpu as pltpu

EPS = 1e-6
D_MODEL = 2048
D_POOL = 512
D_SCONV = 768
D_CCONV = 768
POOL_WINDOWS = (2, 4, 8, 16)
POOL_GROUP = 128
POOL_BUF = 15
SCONV_K = 3
CCONV_K = 31
FFN_CONV_K = 3
D_FF = 5632
D_IN = D_POOL + 3 * D_SCONV + 2 * D_CCONV
O1 = D_POOL
O2 = O1 + D_SCONV
O3 = O2 + D_SCONV
O4 = O3 + D_SCONV
O5 = O4 + D_CCONV

SUBLANES = 8
LANES = 128
MXU_N = 256
VMEM_LIMIT_BYTES = 60 * 1024 * 1024

MIX_TM_PROMPT = 256
MIX_SAMPLE_SEQS = 32
FFN_TM_PROMPT = 1024
FFN_CK = 512
FFN_ROW_SPLITS = 4
CONV_ROW_BLOCK = 32

BF16 = jnp.bfloat16
F32 = jnp.float32


def _round_up(n, m):
    return (n + m - 1) // m * m


def _dot(a, b):
    return jnp.dot(a, b, preferred_element_type=F32)


def _rms_scale(x):
    return lax.rsqrt(jnp.mean(x * x, axis=-1, keepdims=True) + EPS)


def _rows(ref):
    if len(ref.shape) == 2:
        return ref[...]
    return jnp.concatenate([ref[:, t, :] for t in range(ref.shape[1])], axis=0)


def _store_rows(ref, v):
    if len(ref.shape) == 2:
        ref[...] = v
    else:
        n_b = ref.shape[0]
        for t in range(ref.shape[1]):
            ref[:, t, :] = v[t * n_b:(t + 1) * n_b, :]


def _conv_aligned_stride(ext_ref, w_ref, b_ref, out_ref, ln, halo, tm, tstride):
    k = w_ref.shape[0]
    for rb in range(tm // CONV_ROW_BLOCK):
        r0 = rb * CONV_ROW_BLOCK
        acc = jnp.broadcast_to(b_ref[:, ln], (CONV_ROW_BLOCK, LANES))
        for t in range(k):
            start = halo - (k - 1 - t) * tstride + r0
            acc = acc + w_ref[t:t + 1, ln] * ext_ref[start:start + CONV_ROW_BLOCK, ln]
        out_ref[r0:r0 + CONV_ROW_BLOCK, ln] = acc


def _shift_rows(ext_ref, shift_ref, ln):
    row = lax.broadcasted_iota(jnp.int32, (SUBLANES, LANES), 0)
    prev = None
    for q in range(ext_ref.shape[0] // SUBLANES):
        t = ext_ref[q * SUBLANES:(q + 1) * SUBLANES, ln]
        rolled = [pltpu.roll(t, r, axis=0) for r in range(1, SUBLANES)]
        if prev is not None:
            for r in range(1, SUBLANES):
                shift_ref[r - 1, q * SUBLANES:(q + 1) * SUBLANES, ln] = jnp.where(
                    row >= r, rolled[r - 1], prev[r - 1])
        prev = rolled


def _conv_unit_stride(ext_ref, shift_ref, w_ref, b_ref, out_ref, ln, halo, tm):
    k = w_ref.shape[0]
    for rb in range(tm // CONV_ROW_BLOCK):
        r0 = rb * CONV_ROW_BLOCK
        acc = jnp.broadcast_to(b_ref[:, ln], (CONV_ROW_BLOCK, LANES))
        for d in range(k):
            a, r = divmod(d, SUBLANES)
            start = halo + r0 - a * SUBLANES
            src = ext_ref if r == 0 else shift_ref.at[r - 1]
            acc = acc + w_ref[k - 1 - d:k - d, ln] * src[start:start + CONV_ROW_BLOCK, ln]
        out_ref[r0:r0 + CONV_ROW_BLOCK, ln] = acc


def _mixer_body(*refs, tm, tstride, tiles_per_seq, has_state, n_round):
    it = iter(refs)
    x_ref = next(it)
    if has_state:
        st_pool_ref, st_sconv_ref, st_cconv_ref = next(it), next(it), next(it)
    g1_ref, w_in_ref, pool_w_ref, pool_scale_ref = next(it), next(it), next(it), next(it)
    sconv_w_ref, cconv_w_ref, cconv_b_ref, cconv_g_ref, w_out_ref = (
        next(it), next(it), next(it), next(it), next(it))
    round_in_refs = [next(it) for _ in range(n_round)]
    o_ref, pool_o_ref, sconv_o_ref, cconv_o_ref = next(it), next(it), next(it), next(it)
    round_out_refs = [next(it) for _ in range(n_round)]
    h_ref, mix_ref, ext_p_ref, ext_s_ref, ext_c_ref, cc_ref, proj_s_ref = (
        next(it), next(it), next(it), next(it), next(it), next(it), next(it))
    if not has_state:
        shift_ref = next(it)

    s = tstride
    hp = ext_p_ref.shape[0] - tm
    hs = ext_s_ref.shape[0] - tm
    hc = ext_c_ref.shape[0] - tm
    i = pl.program_id(0)

    if has_state:
        ext_p_ref[0:hp, :] = _rows(st_pool_ref)
        ext_s_ref[0:hs, :] = _rows(st_sconv_ref)
        ext_c_ref[0:hc, :] = _rows(st_cconv_ref)
    else:
        @pl.when(i % tiles_per_seq == 0)
        def _():
            ext_p_ref[0:hp, :] = jnp.zeros((hp, D_POOL), F32)
            ext_s_ref[0:hs, :] = jnp.zeros((hs, D_SCONV), F32)
            ext_c_ref[0:hc, :] = jnp.zeros((hc, D_CCONV), F32)

    for src_ref, dst_ref in zip(round_in_refs, round_out_refs):
        dst_ref[...] = src_ref[...].astype(BF16)

    x = _rows(x_ref)
    h_ref[...] = (x * _rms_scale(x) * g1_ref[...]).astype(BF16)

    def in_proj(col0, c):
        return _dot(h_ref[...], w_in_ref[:, col0 + c * MXU_N:col0 + (c + 1) * MXU_N])

    for c in range(D_CCONV // MXU_N):
        ln = slice(c * MXU_N, (c + 1) * MXU_N)
        ext_c_ref[hc:hc + tm, ln] = in_proj(O4, c) * jax.nn.sigmoid(in_proj(O5, c))

    for c in range(3 * D_SCONV // MXU_N):
        proj_s_ref[:, c * MXU_N:(c + 1) * MXU_N] = in_proj(O1, c)
    for c in range(D_POOL // MXU_N):
        ext_p_ref[hp:hp + tm, c * MXU_N:(c + 1) * MXU_N] = in_proj(0, c)

    for lt in range(D_CCONV // LANES):
        ln = slice(lt * LANES, (lt + 1) * LANES)
        if s == 1:
            _shift_rows(ext_c_ref, shift_ref, ln)
            _conv_unit_stride(ext_c_ref, shift_ref, cconv_w_ref, cconv_b_ref, cc_ref, ln, hc, tm)
        else:
            _conv_aligned_stride(ext_c_ref, cconv_w_ref, cconv_b_ref, cc_ref, ln, hc, tm, s)
    cc = cc_ref[...]
    z = cc * _rms_scale(cc) * cconv_g_ref[...]
    mix_ref[:, O2:D_MODEL] = (z * jax.nn.sigmoid(z)).astype(BF16)

    gate_b = proj_s_ref[:, 0:D_SCONV]
    q = proj_s_ref[:, D_SCONV:2 * D_SCONV] * proj_s_ref[:, 2 * D_SCONV:3 * D_SCONV]
    ext_s_ref[hs:hs + tm, :] = q
    conv = sconv_w_ref[SCONV_K - 1:SCONV_K, :] * q
    for t in range(SCONV_K - 1):
        back = (SCONV_K - 1 - t) * s
        conv = conv + sconv_w_ref[t:t + 1, :] * ext_s_ref[hs - back:hs - back + tm, :]
    mix_ref[:, O1:O2] = (gate_b * conv).astype(BF16)

    if not has_state:
        pos = (i % tiles_per_seq) * tm + lax.broadcasted_iota(jnp.int32, (tm, 1), 0)
    for g, k in enumerate(POOL_WINDOWS):
        ln = slice(g * POOL_GROUP, (g + 1) * POOL_GROUP)
        u = ext_p_ref[hp:hp + tm, ln]
        win = u
        for m in range(1, k):
            win = win + ext_p_ref[hp - m * s:hp - m * s + tm, ln]
        if has_state:
            pooled = win / float(k)
        else:
            pooled = win / jnp.minimum(pos + 1, k).astype(F32)
        d = (pooled - u).astype(BF16)
        y = _dot(d, pool_w_ref[g]) * pool_scale_ref[:, ln]
        mix_ref[:, ln] = y.astype(BF16)

    _store_rows(o_ref, _rows(x_ref) + _dot(mix_ref[...], w_out_ref[...]))

    nb_p, nb_s, nb_c = POOL_BUF * s, (SCONV_K - 1) * s, (CCONV_K - 1) * s
    new_p = ext_p_ref[hp + tm - nb_p:hp + tm, :]
    new_s = ext_s_ref[hs + tm - nb_s:hs + tm, :]
    new_c = ext_c_ref[hc + tm - nb_c:hc + tm, :]
    if has_state:
        _store_rows(pool_o_ref, new_p)
        _store_rows(sconv_o_ref, new_s)
        _store_rows(cconv_o_ref, new_c)
    else:
        pool_o_ref[0] = new_p
        sconv_o_ref[0] = new_s
        cconv_o_ref[0] = new_c
        ext_p_ref[0:hp, :] = ext_p_ref[tm:tm + hp, :]
        ext_s_ref[0:hs, :] = ext_s_ref[tm:tm + hs, :]
        ext_c_ref[0:hc, :] = ext_c_ref[tm:tm + hc, :]


def _layer_spec(shape, layer, single_buffer=False):
    idx = lambda *_: (layer,) + (0,) * len(shape)
    if single_buffer:
        return pl.BlockSpec((None,) + shape, idx, pipeline_mode=pl.Buffered(1))
    return pl.BlockSpec((None,) + shape, idx)


def _mixer_call(x, states, w, layer, w_in_b, w_out_b, *, tm, tstride, tiles_per_seq, n_seq,
                round_weights=()):
    has_state = states is not None
    widths = [D_POOL, D_SCONV, D_CCONV]
    bufs = [POOL_BUF, SCONV_K - 1, CCONV_K - 1]
    if has_state:
        n_b, t_steps, _ = x.shape
        n_tiles = n_b // tstride
        halo = [nb * tstride for nb in bufs]
        x_spec = pl.BlockSpec((tstride, t_steps, D_MODEL), lambda i: (i, 0, 0))
    else:
        n_tiles = x.shape[0] // tm
        halo = [_round_up(nb, SUBLANES) for nb in bufs]
        x_spec = pl.BlockSpec((tm, D_MODEL), lambda i: (i, 0))

    in_specs = [x_spec]
    args = [x]
    if has_state:
        for st, nb, wd in zip(states, bufs, widths):
            in_specs.append(pl.BlockSpec((None, tstride, nb, wd), lambda i: (layer, i, 0, 0)))
            args.append(st)
    in_specs += [
        _layer_spec((1, D_MODEL), layer),
        _layer_spec((D_MODEL, D_IN), 0, single_buffer=True),
        _layer_spec((len(POOL_WINDOWS), POOL_GROUP, POOL_GROUP), layer),
        _layer_spec((1, D_POOL), layer),
        _layer_spec((SCONV_K, D_SCONV), layer),
        _layer_spec((CCONV_K, D_CCONV), layer),
        _layer_spec((1, D_CCONV), layer),
        _layer_spec((1, D_CCONV), layer),
        _layer_spec((D_MODEL, D_MODEL), 0, single_buffer=True),
    ]
    args += [w["norm1_g"], w_in_b, w["pool_w"], w["pool_scale"], w["sconv_w"],
             w["cconv_w"], w["cconv_b"], w["cconv_norm_g"], w_out_b]
    slabs = [(wt.shape[1] // n_tiles, wt.shape[2]) for wt, _ in round_weights]
    for (wt, wl), slab in zip(round_weights, slabs):
        in_specs.append(pl.BlockSpec((None,) + slab, lambda i, wl=wl: (wl, i, 0)))
        args.append(wt)

    out_shape = [jax.ShapeDtypeStruct(x.shape, F32)]
    out_specs = [x_spec]
    for nb, wd in zip(bufs, widths):
        if has_state:
            out_shape.append(jax.ShapeDtypeStruct((n_b, nb, wd), F32))
            out_specs.append(pl.BlockSpec((tstride, nb, wd), lambda i: (i, 0, 0)))
        else:
            out_shape.append(jax.ShapeDtypeStruct((n_seq, nb, wd), F32))
            out_specs.append(pl.BlockSpec((1, nb, wd), lambda i: (i // tiles_per_seq, 0, 0)))

    for slab in slabs:
        out_shape.append(jax.ShapeDtypeStruct((slab[0] * n_tiles, slab[1]), BF16))
        out_specs.append(pl.BlockSpec(slab, lambda i: (i, 0)))

    scratch = [
        pltpu.VMEM((tm, D_MODEL), BF16),
        pltpu.VMEM((tm, D_MODEL), BF16),
        pltpu.VMEM((halo[0] + tm, D_POOL), F32),
        pltpu.VMEM((halo[1] + tm, D_SCONV), F32),
        pltpu.VMEM((halo[2] + tm, D_CCONV), F32),
        pltpu.VMEM((tm, D_CCONV), F32),
        pltpu.VMEM((tm, 3 * D_SCONV), F32),
    ]
    if not has_state:
        scratch.append(pltpu.VMEM((SUBLANES - 1, halo[2] + tm, D_CCONV), F32))
    body = functools.partial(_mixer_body, tm=tm, tstride=tstride, tiles_per_seq=tiles_per_seq,
                             has_state=has_state, n_round=len(round_weights))
    return pl.pallas_call(
        body,
        grid=(n_tiles,),
        in_specs=in_specs,
        out_specs=out_specs,
        out_shape=out_shape,
        scratch_shapes=scratch,
        compiler_params=pltpu.CompilerParams(
            dimension_semantics=("arbitrary",), vmem_limit_bytes=VMEM_LIMIT_BYTES),
        name="mixer_state" if has_state else "mixer_prompt",
    )(*args)


def _row_block(ref, r0, rows):
    if len(ref.shape) == 2:
        return ref.at[r0:r0 + rows]
    n_b = ref.shape[0]
    return ref.at[:, r0 // n_b:(r0 + rows) // n_b, :]


def _ffn_body(*refs, tm, tstride, tiles_per_seq, has_state, final_norm):
    it = iter(refs)
    x_ref = next(it)
    if has_state:
        st_a_ref, st_g_ref = next(it), next(it)
    g2_ref, wa_ref, wg_ref, cwa_ref, cwg_ref, wd_ref = (
        next(it), next(it), next(it), next(it), next(it), next(it))
    if final_norm:
        gf_ref = next(it)
    o_ref, new_a_ref, new_g_ref = next(it), next(it), next(it)
    h_ref, ext_ref = next(it), next(it)
    if not has_state:
        halo_ref = next(it)

    s = tstride
    ck = wa_ref.shape[1]
    hl = ext_ref.shape[0] - tm
    rows = tm // FFN_ROW_SPLITS
    i = pl.program_id(0)
    j = pl.program_id(1)
    nj = pl.num_programs(1)

    @pl.when(j == 0)
    def _():
        x = _rows(x_ref)
        h_ref[...] = (x * _rms_scale(x) * g2_ref[...]).astype(BF16)
        _store_rows(o_ref, x)
        if not has_state:
            @pl.when(i % tiles_per_seq == 0)
            def _():
                halo_ref[...] = jnp.zeros(halo_ref.shape, F32)

    if has_state:
        ext_ref[0:hl, 0:ck] = _rows(st_a_ref)
        ext_ref[0:hl, ck:2 * ck] = _rows(st_g_ref)
    else:
        ext_ref[0:hl, :] = halo_ref[j]

    w_a, w_g, w_d = wa_ref[...], wg_ref[...], wd_ref[...]
    for r in range(FFN_ROW_SPLITS):
        r0 = r * rows
        h = h_ref[r0:r0 + rows, :]
        ext_ref[hl + r0:hl + r0 + rows, 0:ck] = _dot(h, w_a)
        ext_ref[hl + r0:hl + r0 + rows, ck:2 * ck] = _dot(h, w_g)

    def conv(cw_ref, r0, lo):
        out = None
        for t in range(FFN_CONV_K):
            start = hl + r0 - (FFN_CONV_K - 1 - t) * s
            term = cw_ref[t:t + 1, :] * ext_ref[start:start + rows, lo:lo + ck]
            out = term if out is None else out + term
        return out

    for r in range(FFN_ROW_SPLITS):
        r0 = r * rows
        c_a = conv(cwa_ref, r0, 0)
        c_g = conv(cwg_ref, r0, ck)
        act = (c_a * jax.nn.sigmoid(c_a) * c_g).astype(BF16)
        o_blk = _row_block(o_ref, r0, rows)
        _store_rows(o_blk, _rows(o_blk) + _dot(act, w_d))

    nb = (FFN_CONV_K - 1) * s
    new_a = ext_ref[hl + tm - nb:hl + tm, 0:ck]
    new_g = ext_ref[hl + tm - nb:hl + tm, ck:2 * ck]
    if has_state:
        _store_rows(new_a_ref, new_a)
        _store_rows(new_g_ref, new_g)
    else:
        seq = i // tiles_per_seq
        new_a_ref[seq, j] = new_a
        new_g_ref[seq, j] = new_g
        halo_ref[j] = ext_ref[tm:tm + hl, :]

    if final_norm:
        @pl.when(j == nj - 1)
        def _():
            y = _rows(o_ref)
            _store_rows(o_ref, y * _rms_scale(y) * gf_ref[...])


def _ffn_call(x, state, w, w_up_b, w_down_b, layer, final_g, *, tm, tstride, tiles_per_seq, n_seq):
    has_state = state is not None
    final_norm = final_g is not None
    ck = FFN_CK
    nj = D_FF // ck
    nb = FFN_CONV_K - 1
    if has_state:
        n_b, t_steps, _ = x.shape
        n_tiles = n_b // tstride
        hl = nb * tstride
        x_block, x_index = (tstride, t_steps, D_MODEL), lambda i, j: (i, 0, 0)
    else:
        n_tiles = x.shape[0] // tm
        hl = _round_up(nb, SUBLANES)
        x_block, x_index = (tm, D_MODEL), lambda i, j: (i, 0)

    in_specs = [pl.BlockSpec(x_block, x_index)]
    args = [x]
    if has_state:
        in_specs += [pl.BlockSpec((None, tstride, nb, ck), lambda i, j: (layer, i, 0, j)),
                     pl.BlockSpec((None, tstride, nb, ck), lambda i, j: (layer, i, 0, nj + j))]
        args += [state, state]
    in_specs += [
        _layer_spec((1, D_MODEL), layer),
        pl.BlockSpec((D_MODEL, ck), lambda i, j: (0, j)),
        pl.BlockSpec((D_MODEL, ck), lambda i, j: (0, nj + j)),
        pl.BlockSpec((None, FFN_CONV_K, ck), lambda i, j: (layer, 0, j)),
        pl.BlockSpec((None, FFN_CONV_K, ck), lambda i, j: (layer, 0, nj + j)),
        pl.BlockSpec((ck, D_MODEL), lambda i, j: (j, 0)),
    ]
    args += [w["norm2_g"], w_up_b, w_up_b, w["ffn_conv_w"], w["ffn_conv_w"], w_down_b]
    if final_norm:
        in_specs.append(pl.BlockSpec((1, D_MODEL), lambda i, j: (0, 0)))
        args.append(final_g)

    out_shape = [jax.ShapeDtypeStruct(x.shape, F32)]
    out_specs = [pl.BlockSpec(x_block, x_index)]
    for _ in range(2):
        if has_state:
            out_shape.append(jax.ShapeDtypeStruct((n_b, nb, D_FF), F32))
            out_specs.append(pl.BlockSpec((tstride, nb, ck), lambda i, j: (i, 0, j)))
        else:
            out_shape.append(jax.ShapeDtypeStruct((n_seq, nj, nb, ck), F32))
            out_specs.append(pl.BlockSpec((n_seq, nj, nb, ck), lambda i, j: (0, 0, 0, 0)))

    scratch = [pltpu.VMEM((tm, D_MODEL), BF16), pltpu.VMEM((hl + tm, 2 * ck), F32)]
    if not has_state:
        scratch.append(pltpu.VMEM((nj, hl, 2 * ck), F32))
    body = functools.partial(_ffn_body, tm=tm, tstride=tstride, tiles_per_seq=tiles_per_seq,
                             has_state=has_state, final_norm=final_norm)
    return pl.pallas_call(
        body,
        grid=(n_tiles, nj),
        in_specs=in_specs,
        out_specs=out_specs,
        out_shape=out_shape,
        scratch_shapes=scratch,
        compiler_params=pltpu.CompilerParams(
            dimension_semantics=("arbitrary", "arbitrary"), vmem_limit_bytes=VMEM_LIMIT_BYTES),
        name="ffn_state" if has_state else "ffn_prompt",
    )(*args)


def kernel(x_prompt, x_sample, state_pool, state_sconv, state_cconv, state_ffn, norm1_g, w_in,
           pool_w, pool_scale, sconv_w, cconv_w, cconv_b, cconv_norm_g, w_out, norm2_g, w_up,
           ffn_conv_w, w_down, final_norm_g):
    batch, seq, _ = x_prompt.shape
    dec_batch, dec_seq, _ = x_sample.shape
    depth = w_in.shape[0]

    w = {
        "norm1_g": norm1_g.reshape(depth, 1, D_MODEL),
        "pool_w": pool_w.astype(BF16),
        "pool_scale": pool_scale.reshape(depth, 1, D_POOL),
        "sconv_w": sconv_w,
        "cconv_w": cconv_w,
        "cconv_b": cconv_b.reshape(depth, 1, D_CCONV),
        "cconv_norm_g": cconv_norm_g.reshape(depth, 1, D_CCONV),
        "norm2_g": norm2_g.reshape(depth, 1, D_MODEL),
        "ffn_conv_w": ffn_conv_w,
    }
    final_g = final_norm_g.reshape(1, D_MODEL)

    xp = x_prompt.reshape(batch * seq, D_MODEL)
    xs = x_sample
    st_mix = [state_pool, state_sconv, state_cconv]
    st_ffn = state_ffn

    w_in_b, w_out_b = w_in[:1].astype(BF16), w_out[:1].astype(BF16)

    outs_p = [[], [], [], []]
    outs_s = [[], [], [], []]
    for l in range(depth):
        last = final_g if l == depth - 1 else None
        rounding = [(w_up, l), (w_down, l)]
        if l + 1 < depth:
            rounding += [(w_in, l + 1), (w_out, l + 1)]

        xp, pool_p, sconv_p, cconv_p, *rounded = _mixer_call(
            xp, None, w, l, w_in_b, w_out_b, tm=MIX_TM_PROMPT, tstride=1,
            tiles_per_seq=seq // MIX_TM_PROMPT, n_seq=batch, round_weights=rounding)
        w_up_b, w_down_b = rounded[:2]
        xp, ffa_p, ffg_p = _ffn_call(
            xp, None, w, w_up_b, w_down_b, l, last, tm=FFN_TM_PROMPT, tstride=1,
            tiles_per_seq=seq // FFN_TM_PROMPT, n_seq=batch)
        unchunk = lambda a: a.transpose(0, 2, 1, 3).reshape(batch, FFN_CONV_K - 1, D_FF)
        outs_p[0].append(pool_p)
        outs_p[1].append(sconv_p)
        outs_p[2].append(cconv_p)
        outs_p[3].append(jnp.concatenate([unchunk(ffa_p), unchunk(ffg_p)], axis=-1))

        xs, pool_s, sconv_s, cconv_s = _mixer_call(
            xs, st_mix, w, l, w_in_b, w_out_b, tm=dec_seq * MIX_SAMPLE_SEQS,
            tstride=MIX_SAMPLE_SEQS, tiles_per_seq=1, n_seq=dec_batch)
        xs, ffa_s, ffg_s = _ffn_call(
            xs, st_ffn, w, w_up_b, w_down_b, l, last, tm=dec_seq * dec_batch, tstride=dec_batch,
            tiles_per_seq=1, n_seq=dec_batch)
        outs_s[0].append(pool_s)
        outs_s[1].append(sconv_s)
        outs_s[2].append(cconv_s)
        outs_s[3].append(jnp.concatenate([ffa_s, ffg_s], axis=-1))
        if l + 1 < depth:
            w_in_b, w_out_b = rounded[2][None], rounded[3][None]

    y_prompt = xp.reshape(batch, seq, D_MODEL)
    return (y_prompt, xs,
            jnp.stack(outs_p[0]), jnp.stack(outs_p[1]), jnp.stack(outs_p[2]), jnp.stack(outs_p[3]),
            jnp.stack(outs_s[0]), jnp.stack(outs_s[1]), jnp.stack(outs_s[2]), jnp.stack(outs_s[3]))
```

```python
import functools

import jax
import jax.numpy as jnp
from jax import lax
from jax.experimental import pallas as pl
from jax.experimental.pallas import tpu as pltpu

EPS = 1e-6
D_MODEL = 2048
D_POOL = 512
D_SCONV = 768
D_CCONV = 768
POOL_WINDOWS = (2, 4, 8, 16)
POOL_GROUP = 128
POOL_BUF = 15
SCONV_K = 3
CCONV_K = 31
FFN_CONV_K = 3
D_FF = 5632
D_IN = D_POOL + 3 * D_SCONV + 2 * D_CCONV
O1 = D_POOL
O2 = O1 + D_SCONV
O3 = O2 + D_SCONV
O4 = O3 + D_SCONV
O5 = O4 + D_CCONV

SUBLANES = 8
LANES = 128
MXU_N = 256
VMEM_LIMIT_BYTES = 60 * 1024 * 1024

MIX_TM_PROMPT = 256
MIX_SAMPLE_SEQS = 32
FFN_TM_PROMPT = 1024
FFN_CK = 512
FFN_ROW_SPLITS = 2
CONV_ROW_BLOCK = 32

BF16 = jnp.bfloat16
F32 = jnp.float32


def _round_up(n, m):
    return (n + m - 1) // m * m


def _dot(a, b):
    return jnp.dot(a, b, preferred_element_type=F32)


def _rms_scale(x):
    return lax.rsqrt(jnp.mean(x * x, axis=-1, keepdims=True) + EPS)


def _rows(ref):
    v = ref[...]
    return v.reshape(-1, v.shape[-1])


def _store_rows(ref, v):
    ref[...] = v.reshape(ref.shape)


def _conv_aligned_stride(ext_ref, w_ref, b_ref, out_ref, ln, halo, tm, tstride):
    k = w_ref.shape[0]
    for rb in range(tm // CONV_ROW_BLOCK):
        r0 = rb * CONV_ROW_BLOCK
        acc = jnp.broadcast_to(b_ref[:, ln], (CONV_ROW_BLOCK, LANES))
        for t in range(k):
            start = halo - (k - 1 - t) * tstride + r0
            acc = acc + w_ref[t:t + 1, ln] * ext_ref[start:start + CONV_ROW_BLOCK, ln]
        out_ref[r0:r0 + CONV_ROW_BLOCK, ln] = acc


def _shift_rows(ext_ref, shift_ref, ln):
    row = lax.broadcasted_iota(jnp.int32, (SUBLANES, LANES), 0)
    prev = None
    for q in range(ext_ref.shape[0] // SUBLANES):
        t = ext_ref[q * SUBLANES:(q + 1) * SUBLANES, ln]
        rolled = [pltpu.roll(t, r, axis=0) for r in range(1, SUBLANES)]
        if prev is not None:
            for r in range(1, SUBLANES):
                shift_ref[r - 1, q * SUBLANES:(q + 1) * SUBLANES, ln] = jnp.where(
                    row >= r, rolled[r - 1], prev[r - 1])
        prev = rolled


def _conv_unit_stride(ext_ref, shift_ref, w_ref, b_ref, out_ref, ln, halo, tm):
    k = w_ref.shape[0]
    for rb in range(tm // CONV_ROW_BLOCK):
        r0 = rb * CONV_ROW_BLOCK
        acc = jnp.broadcast_to(b_ref[:, ln], (CONV_ROW_BLOCK, LANES))
        for d in range(k):
            a, r = divmod(d, SUBLANES)
            start = halo + r0 - a * SUBLANES
            src = ext_ref if r == 0 else shift_ref.at[r - 1]
            acc = acc + w_ref[k - 1 - d:k - d, ln] * src[start:start + CONV_ROW_BLOCK, ln]
        out_ref[r0:r0 + CONV_ROW_BLOCK, ln] = acc


def _mixer_body(*refs, tm, tstride, tiles_per_seq, has_state, n_round):
    it = iter(refs)
    x_ref = next(it)
    if has_state:
        st_pool_ref, st_sconv_ref, st_cconv_ref = next(it), next(it), next(it)
    g1_ref, w_in_ref, pool_w_ref, pool_scale_ref = next(it), next(it), next(it), next(it)
    sconv_w_ref, cconv_w_ref, cconv_b_ref, cconv_g_ref, w_out_ref = (
        next(it), next(it), next(it), next(it), next(it))
    round_in_refs = [next(it) for _ in range(n_round)]
    o_ref, pool_o_ref, sconv_o_ref, cconv_o_ref = next(it), next(it), next(it), next(it)
    round_out_refs = [next(it) for _ in range(n_round)]
    h_ref, mix_ref, ext_p_ref, ext_s_ref, ext_c_ref, cc_ref, proj_s_ref = (
        next(it), next(it), next(it), next(it), next(it), next(it), next(it))
    if not has_state:
        shift_ref = next(it)

    s = tstride
    hp = ext_p_ref.shape[0] - tm
    hs = ext_s_ref.shape[0] - tm
    hc = ext_c_ref.shape[0] - tm
    i = pl.program_id(0)

    if has_state:
        ext_p_ref[0:hp, :] = _rows(st_pool_ref)
        ext_s_ref[0:hs, :] = _rows(st_sconv_ref)
        ext_c_ref[0:hc, :] = _rows(st_cconv_ref)
    else:
        @pl.when(i % tiles_per_seq == 0)
        def _():
            ext_p_ref[0:hp, :] = jnp.zeros((hp, D_POOL), F32)
            ext_s_ref[0:hs, :] = jnp.zeros((hs, D_SCONV), F32)
            ext_c_ref[0:hc, :] = jnp.zeros((hc, D_CCONV), F32)

    for src_ref, dst_ref in zip(round_in_refs, round_out_refs):
        dst_ref[...] = src_ref[...].astype(BF16)

    x = _rows(x_ref)
    h_ref[...] = (x * _rms_scale(x) * g1_ref[...]).astype(BF16)

    def in_proj(col0, c):
        return _dot(h_ref[...], w_in_ref[:, col0 + c * MXU_N:col0 + (c + 1) * MXU_N])

    for c in range(D_CCONV // MXU_N):
        ln = slice(c * MXU_N, (c + 1) * MXU_N)
        ext_c_ref[hc:hc + tm, ln] = in_proj(O4, c) * jax.nn.sigmoid(in_proj(O5, c))

    for c in range(3 * D_SCONV // MXU_N):
        proj_s_ref[:, c * MXU_N:(c + 1) * MXU_N] = in_proj(O1, c)
    for c in range(D_POOL // MXU_N):
        ext_p_ref[hp:hp + tm, c * MXU_N:(c + 1) * MXU_N] = in_proj(0, c)

    for lt in range(D_CCONV // LANES):
        ln = slice(lt * LANES, (lt + 1) * LANES)
        if s == 1:
            _shift_rows(ext_c_ref, shift_ref, ln)
            _conv_unit_stride(ext_c_ref, shift_ref, cconv_w_ref, cconv_b_ref, cc_ref, ln, hc, tm)
        else:
            _conv_aligned_stride(ext_c_ref, cconv_w_ref, cconv_b_ref, cc_ref, ln, hc, tm, s)
    cc = cc_ref[...]
    z = cc * _rms_scale(cc) * cconv_g_ref[...]
    mix_ref[:, O2:D_MODEL] = (z * jax.nn.sigmoid(z)).astype(BF16)

    gate_b = proj_s_ref[:, 0:D_SCONV]
    q = proj_s_ref[:, D_SCONV:2 * D_SCONV] * proj_s_ref[:, 2 * D_SCONV:3 * D_SCONV]
    ext_s_ref[hs:hs + tm, :] = q
    conv = sconv_w_ref[SCONV_K - 1:SCONV_K, :] * q
    for t in range(SCONV_K - 1):
        back = (SCONV_K - 1 - t) * s
        conv = conv + sconv_w_ref[t:t + 1, :] * ext_s_ref[hs - back:hs - back + tm, :]
    mix_ref[:, O1:O2] = (gate_b * conv).astype(BF16)

    if not has_state:
        pos = (i % tiles_per_seq) * tm + lax.broadcasted_iota(jnp.int32, (tm, 1), 0)
    for g, k in enumerate(POOL_WINDOWS):
        ln = slice(g * POOL_GROUP, (g + 1) * POOL_GROUP)
        u = ext_p_ref[hp:hp + tm, ln]
        win = u
        for m in range(1, k):
            win = win + ext_p_ref[hp - m * s:hp - m * s + tm, ln]
        if has_state:
            pooled = win / float(k)
        else:
            pooled = win / jnp.minimum(pos + 1, k).astype(F32)
        d = (pooled - u).astype(BF16)
        y = _dot(d, pool_w_ref[g]) * pool_scale_ref[:, ln]
        mix_ref[:, ln] = y.astype(BF16)

    _store_rows(o_ref, _rows(x_ref) + _dot(mix_ref[...], w_out_ref[...]))

    nb_p, nb_s, nb_c = POOL_BUF * s, (SCONV_K - 1) * s, (CCONV_K - 1) * s
    new_p = ext_p_ref[hp + tm - nb_p:hp + tm, :]
    new_s = ext_s_ref[hs + tm - nb_s:hs + tm, :]
    new_c = ext_c_ref[hc + tm - nb_c:hc + tm, :]
    if has_state:
        _store_rows(pool_o_ref, new_p)
        _store_rows(sconv_o_ref, new_s)
        _store_rows(cconv_o_ref, new_c)
    else:
        pool_o_ref[0] = new_p
        sconv_o_ref[0] = new_s
        cconv_o_ref[0] = new_c
        ext_p_ref[0:hp, :] = ext_p_ref[tm:tm + hp, :]
        ext_s_ref[0:hs, :] = ext_s_ref[tm:tm + hs, :]
        ext_c_ref[0:hc, :] = ext_c_ref[tm:tm + hc, :]


def _layer_spec(shape, layer, single_buffer=False):
    idx = lambda *_: (layer,) + (0,) * len(shape)
    if single_buffer:
        return pl.BlockSpec((None,) + shape, idx, pipeline_mode=pl.Buffered(1))
    return pl.BlockSpec((None,) + shape, idx)


def _mixer_call(x, states, w, layer, w_in_b, w_out_b, *, tm, tstride, tiles_per_seq, n_seq,
                round_weights=()):
    has_state = states is not None
    widths = [D_POOL, D_SCONV, D_CCONV]
    bufs = [POOL_BUF, SCONV_K - 1, CCONV_K - 1]
    if has_state:
        t_steps, n_b, _ = x.shape
        n_tiles = n_b // tstride
        halo = [nb * tstride for nb in bufs]
        x_spec = pl.BlockSpec((t_steps, tstride, D_MODEL), lambda i: (0, i, 0))
    else:
        n_tiles = x.shape[0] // tm
        halo = [_round_up(nb, SUBLANES) for nb in bufs]
        x_spec = pl.BlockSpec((tm, D_MODEL), lambda i: (i, 0))

    in_specs = [x_spec]
    args = [x]
    if has_state:
        for st, nb, wd in zip(states, bufs, widths):
            in_specs.append(pl.BlockSpec((None, nb, tstride, wd), lambda i: (layer, 0, i, 0)))
            args.append(st)
    in_specs += [
        _layer_spec((1, D_MODEL), layer),
        _layer_spec((D_MODEL, D_IN), 0, single_buffer=True),
        _layer_spec((len(POOL_WINDOWS), POOL_GROUP, POOL_GROUP), layer),
        _layer_spec((1, D_POOL), layer),
        _layer_spec((SCONV_K, D_SCONV), layer),
        _layer_spec((CCONV_K, D_CCONV), layer),
        _layer_spec((1, D_CCONV), layer),
        _layer_spec((1, D_CCONV), layer),
        _layer_spec((D_MODEL, D_MODEL), 0, single_buffer=True),
    ]
    args += [w["norm1_g"], w_in_b, w["pool_w"], w["pool_scale"], w["sconv_w"],
             w["cconv_w"], w["cconv_b"], w["cconv_norm_g"], w_out_b]
    slabs = [(wt.shape[1] // n_tiles, wt.shape[2]) for wt, _ in round_weights]
    for (wt, wl), slab in zip(round_weights, slabs):
        in_specs.append(pl.BlockSpec((None,) + slab, lambda i, wl=wl: (wl, i, 0)))
        args.append(wt)

    out_shape = [jax.ShapeDtypeStruct(x.shape, F32)]
    out_specs = [x_spec]
    for nb, wd in zip(bufs, widths):
        if has_state:
            out_shape.append(jax.ShapeDtypeStruct((nb, n_b, wd), F32))
            out_specs.append(pl.BlockSpec((nb, tstride, wd), lambda i: (0, i, 0)))
        else:
            out_shape.append(jax.ShapeDtypeStruct((n_seq, nb, wd), F32))
            out_specs.append(pl.BlockSpec((1, nb, wd), lambda i: (i // tiles_per_seq, 0, 0)))

    for slab in slabs:
        out_shape.append(jax.ShapeDtypeStruct((slab[0] * n_tiles, slab[1]), BF16))
        out_specs.append(pl.BlockSpec(slab, lambda i: (i, 0)))

    scratch = [
        pltpu.VMEM((tm, D_MODEL), BF16),
        pltpu.VMEM((tm, D_MODEL), BF16),
        pltpu.VMEM((halo[0] + tm, D_POOL), F32),
        pltpu.VMEM((halo[1] + tm, D_SCONV), F32),
        pltpu.VMEM((halo[2] + tm, D_CCONV), F32),
        pltpu.VMEM((tm, D_CCONV), F32),
        pltpu.VMEM((tm, 3 * D_SCONV), F32),
    ]
    if not has_state:
        scratch.append(pltpu.VMEM((SUBLANES - 1, halo[2] + tm, D_CCONV), F32))
    body = functools.partial(_mixer_body, tm=tm, tstride=tstride, tiles_per_seq=tiles_per_seq,
                             has_state=has_state, n_round=len(round_weights))
    return pl.pallas_call(
        body,
        grid=(n_tiles,),
        in_specs=in_specs,
        out_specs=out_specs,
        out_shape=out_shape,
        scratch_shapes=scratch,
        compiler_params=pltpu.CompilerParams(
            dimension_semantics=("arbitrary",), vmem_limit_bytes=VMEM_LIMIT_BYTES),
        name="mixer_state" if has_state else "mixer_prompt",
    )(*args)


def _row_block(ref, r0, rows):
    if len(ref.shape) == 2:
        return ref.at[r0:r0 + rows]
    n_b = ref.shape[1]
    return ref.at[r0 // n_b:(r0 + rows) // n_b]


def _ffn_body(*refs, tm, tstride, tiles_per_seq, has_state, final_norm):
    it = iter(refs)
    x_ref = next(it)
    if has_state:
        st_a_ref, st_g_ref = next(it), next(it)
    g2_ref, wa_ref, wg_ref, cwa_ref, cwg_ref, wd_ref = (
        next(it), next(it), next(it), next(it), next(it), next(it))
    if final_norm:
        gf_ref = next(it)
    o_ref, new_a_ref, new_g_ref = next(it), next(it), next(it)
    h_ref, ext_ref = next(it), next(it)
    if not has_state:
        halo_ref = next(it)

    s = tstride
    ck = wa_ref.shape[1]
    hl = ext_ref.shape[0] - tm
    rows = tm // FFN_ROW_SPLITS
    i = pl.program_id(0)
    j = pl.program_id(1)
    nj = pl.num_programs(1)

    @pl.when(j == 0)
    def _():
        x = _rows(x_ref)
        h_ref[...] = (x * _rms_scale(x) * g2_ref[...]).astype(BF16)
        _store_rows(o_ref, x)
        if not has_state:
            @pl.when(i % tiles_per_seq == 0)
            def _():
                halo_ref[...] = jnp.zeros(halo_ref.shape, F32)

    if has_state:
        ext_ref[0:hl, 0:ck] = _rows(st_a_ref)
        ext_ref[0:hl, ck:2 * ck] = _rows(st_g_ref)
    else:
        ext_ref[0:hl, :] = halo_ref[j]

    w_a, w_g, w_d = wa_ref[...], wg_ref[...], wd_ref[...]
    for r in range(FFN_ROW_SPLITS):
        r0 = r * rows
        h = h_ref[r0:r0 + rows, :]
        ext_ref[hl + r0:hl + r0 + rows, 0:ck] = _dot(h, w_a)
        ext_ref[hl + r0:hl + r0 + rows, ck:2 * ck] = _dot(h, w_g)

    def conv(cw_ref, r0, lo):
        out = None
        for t in range(FFN_CONV_K):
            start = hl + r0 - (FFN_CONV_K - 1 - t) * s
            term = cw_ref[t:t + 1, :] * ext_ref[start:start + rows, lo:lo + ck]
            out = term if out is None else out + term
        return out

    for r in range(FFN_ROW_SPLITS):
        r0 = r * rows
        c_a = conv(cwa_ref, r0, 0)
        c_g = conv(cwg_ref, r0, ck)
        act = (c_a * jax.nn.sigmoid(c_a) * c_g).astype(BF16)
        o_blk = _row_block(o_ref, r0, rows)
        _store_rows(o_blk, _rows(o_blk) + _dot(act, w_d))

    nb = (FFN_CONV_K - 1) * s
    new_a = ext_ref[hl + tm - nb:hl + tm, 0:ck]
    new_g = ext_ref[hl + tm - nb:hl + tm, ck:2 * ck]
    if has_state:
        _store_rows(new_a_ref, new_a)
        _store_rows(new_g_ref, new_g)
    else:
        seq = i // tiles_per_seq
        new_a_ref[seq, j] = new_a
        new_g_ref[seq, j] = new_g
        halo_ref[j] = ext_ref[tm:tm + hl, :]

    if final_norm:
        @pl.when(j == nj - 1)
        def _():
            y = _rows(o_ref)
            _store_rows(o_ref, y * _rms_scale(y) * gf_ref[...])


def _ffn_call(x, state, w, w_up_b, w_down_b, layer, final_g, *, tm, tstride, tiles_per_seq, n_seq):
    has_state = state is not None
    final_norm = final_g is not None
    ck = FFN_CK
    nj = D_FF // ck
    nb = FFN_CONV_K - 1
    if has_state:
        t_steps, n_b, _ = x.shape
        n_tiles = n_b // tstride
        hl = nb * tstride
        x_block, x_index = (t_steps, tstride, D_MODEL), lambda i, j: (0, i, 0)
    else:
        n_tiles = x.shape[0] // tm
        hl = _round_up(nb, SUBLANES)
        x_block, x_index = (tm, D_MODEL), lambda i, j: (i, 0)

    in_specs = [pl.BlockSpec(x_block, x_index)]
    args = [x]
    if has_state:
        in_specs += [pl.BlockSpec((None, nb, tstride, ck), lambda i, j: (layer, 0, i, j)),
                     pl.BlockSpec((None, nb, tstride, ck), lambda i, j: (layer, 0, i, nj + j))]
        args += [state, state]
    in_specs += [
        _layer_spec((1, D_MODEL), layer),
        pl.BlockSpec((D_MODEL, ck), lambda i, j: (0, j)),
        pl.BlockSpec((D_MODEL, ck), lambda i, j: (0, nj + j)),
        pl.BlockSpec((None, FFN_CONV_K, ck), lambda i, j: (layer, 0, j)),
        pl.BlockSpec((None, FFN_CONV_K, ck), lambda i, j: (layer, 0, nj + j)),
        pl.BlockSpec((ck, D_MODEL), lambda i, j: (j, 0)),
    ]
    args += [w["norm2_g"], w_up_b, w_up_b, w["ffn_conv_w"], w["ffn_conv_w"], w_down_b]
    if final_norm:
        in_specs.append(pl.BlockSpec((1, D_MODEL), lambda i, j: (0, 0)))
        args.append(final_g)

    out_shape = [jax.ShapeDtypeStruct(x.shape, F32)]
    out_specs = [pl.BlockSpec(x_block, x_index)]
    for _ in range(2):
        if has_state:
            out_shape.append(jax.ShapeDtypeStruct((nb, n_b, D_FF), F32))
            out_specs.append(pl.BlockSpec((nb, tstride, ck), lambda i, j: (0, i, j)))
        else:
            out_shape.append(jax.ShapeDtypeStruct((n_seq, nj, nb, ck), F32))
            out_specs.append(pl.BlockSpec((n_seq, nj, nb, ck), lambda i, j: (0, 0, 0, 0)))

    scratch = [pltpu.VMEM((tm, D_MODEL), BF16), pltpu.VMEM((hl + tm, 2 * ck), F32)]
    if not has_state:
        scratch.append(pltpu.VMEM((nj, hl, 2 * ck), F32))
    body = functools.partial(_ffn_body, tm=tm, tstride=tstride, tiles_per_seq=tiles_per_seq,
                             has_state=has_state, final_norm=final_norm)
    return pl.pallas_call(
        body,
        grid=(n_tiles, nj),
        in_specs=in_specs,
        out_specs=out_specs,
        out_shape=out_shape,
        scratch_shapes=scratch,
        compiler_params=pltpu.CompilerParams(
            dimension_semantics=("arbitrary", "arbitrary"), vmem_limit_bytes=VMEM_LIMIT_BYTES),
        name="ffn_state" if has_state else "ffn_prompt",
    )(*args)


def _time_major(a):
    return jnp.swapaxes(a, -3, -2)


def kernel(x_prompt, x_sample, state_pool, state_sconv, state_cconv, state_ffn, norm1_g, w_in,
           pool_w, pool_scale, sconv_w, cconv_w, cconv_b, cconv_norm_g, w_out, norm2_g, w_up,
           ffn_conv_w, w_down, final_norm_g):
    batch, seq, _ = x_prompt.shape
    dec_batch, dec_seq, _ = x_sample.shape
    depth = w_in.shape[0]

    w = {
        "norm1_g": norm1_g.reshape(depth, 1, D_MODEL),
        "pool_w": pool_w.astype(BF16),
        "pool_scale": pool_scale.reshape(depth, 1, D_POOL),
        "sconv_w": sconv_w,
        "cconv_w": cconv_w,
        "cconv_b": cconv_b.reshape(depth, 1, D_CCONV),
        "cconv_norm_g": cconv_norm_g.reshape(depth, 1, D_CCONV),
        "norm2_g": norm2_g.reshape(depth, 1, D_MODEL),
        "ffn_conv_w": ffn_conv_w,
    }
    final_g = final_norm_g.reshape(1, D_MODEL)

    xp = x_prompt.reshape(batch * seq, D_MODEL)
    xs = _time_major(x_sample)
    st_mix = [_time_major(state_pool), _time_major(state_sconv), _time_major(state_cconv)]
    st_ffn = _time_major(state_ffn)

    w_in_b, w_out_b = w_in[:1].astype(BF16), w_out[:1].astype(BF16)

    outs_p = [[], [], [], []]
    outs_s = [[], [], [], []]
    for l in range(depth):
        last = final_g if l == depth - 1 else None
        rounding = [(w_up, l), (w_down, l)]
        if l + 1 < depth:
            rounding += [(w_in, l + 1), (w_out, l + 1)]

        xp, pool_p, sconv_p, cconv_p, *rounded = _mixer_call(
            xp, None, w, l, w_in_b, w_out_b, tm=MIX_TM_PROMPT, tstride=1,
            tiles_per_seq=seq // MIX_TM_PROMPT, n_seq=batch, round_weights=rounding)
        w_up_b, w_down_b = rounded[:2]
        xp, ffa_p, ffg_p = _ffn_call(
            xp, None, w, w_up_b, w_down_b, l, last, tm=FFN_TM_PROMPT, tstride=1,
            tiles_per_seq=seq // FFN_TM_PROMPT, n_seq=batch)
        unchunk = lambda a: a.transpose(0, 2, 1, 3).reshape(batch, FFN_CONV_K - 1, D_FF)
        outs_p[0].append(pool_p)
        outs_p[1].append(sconv_p)
        outs_p[2].append(cconv_p)
        outs_p[3].append(jnp.concatenate([unchunk(ffa_p), unchunk(ffg_p)], axis=-1))

        xs, pool_s, sconv_s, cconv_s = _mixer_call(
            xs, st_mix, w, l, w_in_b, w_out_b, tm=dec_seq * MIX_SAMPLE_SEQS,
            tstride=MIX_SAMPLE_SEQS, tiles_per_seq=1, n_seq=dec_batch)
        xs, ffa_s, ffg_s = _ffn_call(
            xs, st_ffn, w, w_up_b, w_down_b, l, last, tm=dec_seq * dec_batch, tstride=dec_batch,
            tiles_per_seq=1, n_seq=dec_batch)
        outs_s[0].append(pool_s)
        outs_s[1].append(sconv_s)
        outs_s[2].append(cconv_s)
        outs_s[3].append(jnp.concatenate([ffa_s, ffg_s], axis=-1))
        if l + 1 < depth:
            w_in_b, w_out_b = rounded[2][None], rounded[3][None]

    y_prompt = xp.reshape(batch, seq, D_MODEL)
    y_sample = _time_major(xs)
    stack_tm = lambda parts: _time_major(jnp.stack(parts))
    return (y_prompt, y_sample,
            jnp.stack(outs_p[0]), jnp.stack(outs_p[1]), jnp.stack(outs_p[2]), jnp.stack(outs_p[3]),
            stack_tm(outs_s[0]), stack_tm(outs_s[1]), stack_tm(outs_s[2]), stack_tm(outs_s[3]))
```
